```python
import jax, jax.numpy as jnp
from jax import lax
import numpy as np

D_MODEL = 1024
BATCH = 4
SEQ = 8192
DEPTH = 4

CHUNK = 64
PE_DIM = 256
EPS = 1e-6
M_HEADS = 4
M_DK = 256
M_DV = 256
M_QK = M_HEADS * M_DK
M_V = M_HEADS * M_DV
CONV_K = 4
R_HEADS = 4
R_DK = 256
R_DV = 512
R_QK = R_HEADS * R_DK
R_V = R_HEADS * R_DV
ROPE_BASE = 10000.0
D_FF = 4 * D_MODEL
IN_SIZES = (M_QK, M_QK, M_V, M_V, M_HEADS, M_HEADS, R_QK, R_QK, R_V, R_V, D_MODEL, D_MODEL)
N_IN = M_QK * 2 + M_V * 2 + M_HEADS * 2 + R_QK * 2 + R_V * 2 + D_MODEL * 2
F_GATE_OFFSET = M_QK * 2 + M_V * 2 + M_HEADS

kernel_name = "hybrid_mlstm_retention_griffin_block"


def split_cols(z):
    outs = []
    off = 0
    for s in IN_SIZES:
        outs.append(z[..., off:off + s])
        off += s
    return outs


def rmsnorm(x, g):
    xf = x.astype(jnp.float32)
    y = xf * lax.rsqrt(jnp.mean(xf * xf, axis=-1, keepdims=True) + EPS)
    return (y * g.astype(jnp.float32)).astype(x.dtype)


def head_layernorm(y, g, n_heads):
    B, S, W = y.shape
    yf = y.astype(jnp.float32).reshape(B, S, n_heads, W // n_heads)
    mu = jnp.mean(yf, axis=-1, keepdims=True)
    var = jnp.mean(jnp.square(yf - mu), axis=-1, keepdims=True)
    yn = ((yf - mu) * lax.rsqrt(var + EPS)).reshape(B, S, W)
    return yn * g.astype(jnp.float32)


def causal_conv(u, w, b):
    S = u.shape[1]
    up = jnp.pad(u, ((0, 0), (CONV_K - 1, 0), (0, 0)))
    out = b
    for j in range(CONV_K):
        out = out + up[:, j:j + S, :] * w[j]
    return out


def rope(x, cos, sin):
    half = x.shape[-1] // 2
    x1, x2 = x[..., :half], x[..., half:]
    return jnp.concatenate([x1 * cos - x2 * sin, x1 * sin + x2 * cos], axis=-1)


def to_chunks(x, n_heads):
    B, S, W = x.shape
    return x.reshape(B, S // CHUNK, CHUNK, n_heads, W // n_heads).transpose(1, 0, 3, 2, 4)


def gates_to_chunks(g):
    B, S, H = g.shape
    return g.reshape(B, S // CHUNK, CHUNK, H).transpose(1, 0, 3, 2)


def from_chunks(y):
    NC, B, H, L, d = y.shape
    return y.transpose(1, 0, 3, 2, 4).reshape(B, NC * L, H * d)


def mlstm_chunkwise(q, k, v, li, lf):
    NC, B, H, L, dk = q.shape
    dv = v.shape[-1]
    causal = jnp.tril(jnp.ones((L, L), dtype=bool))

    def step(carry, inp):
        C, n, m = carry
        qc, kc, vc, lic, lfc = inp
        b = jnp.cumsum(lfc, axis=-1)
        log_intra = b[..., :, None] - b[..., None, :] + lic[..., None, :]
        log_intra = jnp.where(causal, log_intra, -jnp.inf)
        log_inter = b + m[..., None]
        m_t = jnp.maximum(log_inter, jnp.max(log_intra, axis=-1))
        d_intra = jnp.exp(log_intra - m_t[..., None])
        d_inter = jnp.exp(log_inter - m_t)
        s = jnp.einsum('bhtd,bhsd->bhts', qc, kc) * d_intra
        num = jnp.einsum('bhts,bhsv->bhtv', s, vc) + d_inter[..., None] * jnp.einsum('bhtd,bhdv->bhtv', qc, C)
        den = jnp.sum(s, axis=-1) + d_inter * jnp.einsum('bhtd,bhd->bht', qc, n)
        h = num / jnp.maximum(jnp.abs(den), jnp.exp(-m_t))[..., None]
        b_end = b[..., -1]
        log_w = b_end[..., None] - b + lic
        m_new = jnp.maximum(b_end + m, jnp.max(log_w, axis=-1))
        w = jnp.exp(log_w - m_new[..., None])
        decay = jnp.exp(b_end + m - m_new)
        C = decay[..., None, None] * C + jnp.einsum('bhsd,bhsv->bhdv', kc * w[..., None], vc)
        n = decay[..., None] * n + jnp.einsum('bhs,bhsd->bhd', w, kc)
        return (C, n, m_new), h

    init = (jnp.zeros((B, H, dk, dv), jnp.float32), jnp.zeros((B, H, dk), jnp.float32),
            jnp.zeros((B, H), jnp.float32))
    _, hs = lax.scan(step, init, (q, k, v, li, lf))
    return hs


def retention_chunkwise(q, k, v, log_gamma):
    NC, B, H, L, dk = q.shape
    dv = v.shape[-1]
    pos = jnp.arange(L, dtype=jnp.float32)
    diff = pos[:, None] - pos[None, :]
    lg = log_gamma[:, None, None]
    intra = jnp.where(diff >= 0, jnp.exp(jnp.maximum(diff, 0.0) * lg), 0.0)
    inter = jnp.exp((pos + 1.0) * log_gamma[:, None])
    kdec = jnp.exp((L - 1.0 - pos) * log_gamma[:, None])
    cdec = jnp.exp(L * log_gamma)

    def step(R, inp):
        qc, kc, vc = inp
        s = jnp.einsum('bhtd,bhsd->bhts', qc, kc) * intra
        y = jnp.einsum('bhts,bhsv->bhtv', s, vc) + jnp.einsum('bhtd,bhdv->bhtv', qc, R) * inter[..., None]
        R = cdec[:, None, None] * R + jnp.einsum('bhsd,bhsv->bhdv', kc * kdec[..., None], vc)
        return R, y

    _, ys = lax.scan(step, jnp.zeros((B, H, dk, dv), jnp.float32), (q, k, v))
    return ys


def setup_inputs(seed: int = 0) -> dict:
    key = jax.random.key(seed)
    ks = jax.random.split(key, 24)
    f32 = jnp.float32

    def nrm(k, shape, scale):
        return jax.random.normal(k, shape, f32) * scale

    def gain(k, shape):
        return 1.0 + 0.05 * jax.random.normal(k, shape, f32)

    b_in = nrm(ks[4], (DEPTH, N_IN), 0.02)
    f_bias = jnp.linspace(3.0, 6.0, M_HEADS, dtype=f32)[None, :] + nrm(ks[5], (DEPTH, M_HEADS), 0.1)
    b_in = b_in.at[:, F_GATE_OFFSET:F_GATE_OFFSET + M_HEADS].set(f_bias)
    return {
        "x": nrm(ks[0], (BATCH, SEQ, D_MODEL), 1.0),
        "p": nrm(ks[1], (DEPTH, BATCH, SEQ, PE_DIM), 1.0),
        "norm1_g": gain(ks[2], (DEPTH, D_MODEL)),
        "w_in": nrm(ks[3], (DEPTH, D_MODEL, N_IN), D_MODEL ** -0.5),
        "b_in": b_in,
        "conv_w": nrm(ks[6], (DEPTH, CONV_K, 2 * M_QK), CONV_K ** -0.5),
        "conv_b": nrm(ks[7], (DEPTH, 2 * M_QK), 0.02),
        "m_norm_g": gain(ks[8], (DEPTH, M_V)),
        "r_norm_g": gain(ks[9], (DEPTH, R_V)),
        "w_bm": nrm(ks[10], (DEPTH, M_V, D_MODEL), M_V ** -0.5),
        "w_br": nrm(ks[11], (DEPTH, R_V, D_MODEL), R_V ** -0.5),
        "w_out": nrm(ks[12], (DEPTH, D_MODEL, D_MODEL), D_MODEL ** -0.5),
        "norm2_g": gain(ks[13], (DEPTH, D_MODEL)),
        "w_ff1": nrm(ks[14], (DEPTH, D_MODEL, D_FF), D_MODEL ** -0.5),
        "b_ff1": nrm(ks[15], (DEPTH, D_FF), 0.02),
        "w_ff2": nrm(ks[16], (DEPTH, D_FF, D_MODEL), D_FF ** -0.5),
        "b_ff2": nrm(ks[17], (DEPTH, D_MODEL), 0.02),
        "norm3_g": gain(ks[18], (DEPTH, D_MODEL)),
        "w_pe_gate": nrm(ks[19], (DEPTH, D_MODEL, D_MODEL), D_MODEL ** -0.5),
        "w_pe": nrm(ks[20], (DEPTH, PE_DIM, D_MODEL), PE_DIM ** -0.5),
        "final_g": gain(ks[21], (D_MODEL,)),
    }


def reference(x, p, norm1_g, w_in, b_in, conv_w, conv_b, m_norm_g, r_norm_g, w_bm, w_br,
              w_out, norm2_g, w_ff1, b_ff1, w_ff2, b_ff2, norm3_g, w_pe_gate, w_pe, final_g):
    B, S, _ = x.shape
    dt = x.dtype
    f32 = jnp.float32
    pos = jnp.arange(S, dtype=f32)
    inv_freq = ROPE_BASE ** (-jnp.arange(0, R_DK, 2, dtype=f32) / R_DK)
    ang = pos[:, None] * inv_freq[None, :]
    cos = jnp.cos(ang)[:, None, :]
    sin = jnp.sin(ang)[:, None, :]
    log_gamma = jnp.log(1.0 - jnp.exp2(-5.0 - jnp.arange(R_HEADS, dtype=f32)))

    for i in range(DEPTH):
        h = rmsnorm(x, norm1_g[i])
        z = h @ w_in[i] + b_in[i]
        _, _, mv, mo, mi, mf, rq, rk, rv, rg, gm, gr = split_cols(z)
        qk = jax.nn.silu(causal_conv(z[..., :2 * M_QK], conv_w[i], conv_b[i]))
        mq = qk[..., :M_QK] * (M_DK ** -0.5)
        mk = qk[..., M_QK:]

        hm = mlstm_chunkwise(
            to_chunks(mq.astype(f32), M_HEADS), to_chunks(mk.astype(f32), M_HEADS),
            to_chunks(mv.astype(f32), M_HEADS),
            gates_to_chunks(mi.astype(f32)), gates_to_chunks(jax.nn.log_sigmoid(mf.astype(f32))))
        hm = from_chunks(hm)
        y_m = (jax.nn.sigmoid(mo.astype(f32)) * head_layernorm(hm, m_norm_g[i], M_HEADS)).astype(dt)

        rq4 = rope(rq.astype(f32).reshape(B, S, R_HEADS, R_DK), cos, sin)
        rk4 = rope(rk.astype(f32).reshape(B, S, R_HEADS, R_DK), cos, sin) * (R_DK ** -0.5)
        hr = retention_chunkwise(
            to_chunks(rq4.reshape(B, S, R_QK), R_HEADS), to_chunks(rk4.reshape(B, S, R_QK), R_HEADS),
            to_chunks(rv.astype(f32), R_HEADS), log_gamma)
        hr = from_chunks(hr)
        y_r = (jax.nn.silu(rg.astype(f32)) * head_layernorm(hr, r_norm_g[i], R_HEADS)).astype(dt)

        merged = jax.nn.sigmoid(gm) * (y_m @ w_bm[i]) + jax.nn.sigmoid(gr) * (y_r @ w_br[i])
        x = x + merged @ w_out[i]

        h2 = rmsnorm(x, norm2_g[i])
        x = x + jnp.square(jax.nn.relu(h2 @ w_ff1[i] + b_ff1[i])) @ w_ff2[i] + b_ff2[i]

        h3 = rmsnorm(x, norm3_g[i])
        x = x + jax.nn.sigmoid(h3 @ w_pe_gate[i]) * (p[i] @ w_pe[i])

    return rmsnorm(x, final_g)
```

```python
import functools

import numpy as np
import jax
import jax.numpy as jnp
from jax import lax
from jax.experimental import pallas as pl
from jax.experimental.pallas import tpu as pltpu

F32 = jnp.float32
BF16 = jnp.bfloat16

D_MODEL = 1024
PE_DIM = 256
EPS = 1e-6
HEADS = 4
M_DK = 256
M_DV = 256
R_DK = 256
R_DV = 512
M_QK = HEADS * M_DK
M_V = HEADS * M_DV
R_QK = HEADS * R_DK
R_V = HEADS * R_DV
CONV_K = 4
ROPE_BASE = 10000.0
D_FF = 4 * D_MODEL

Z_MQ, Z_MK, Z_MV, Z_MO = 0, 1024, 2048, 3072
Z_RQ, Z_RK, Z_RV, Z_RG = 4096, 5120, 6144, 8192
Z_GM, Z_GR = 10240, 11264
Z_COLS = 12288
GATE_LANES = 128
GATE_ROWS = 8

LANE = 128
VMEM_LIMIT = 56 * 1024 * 1024

TM_IN = 1024
TN_IN = 1024
CHUNK_L = 256
TM_POST = 256
STRIP = 16


def _sigmoid(x):
    return 1.0 / (1.0 + jnp.exp(-x))


def _log_sigmoid(x):
    return jnp.minimum(x, 0.0) - jnp.log1p(jnp.exp(-jnp.abs(x)))


def _split3(x):
    hi = x.astype(BF16)
    r1 = x - hi.astype(F32)
    mid = r1.astype(BF16)
    lo = (r1 - mid.astype(F32)).astype(BF16)
    return hi, mid, lo


def _dot01_left(mat01, x):
    return sum(jnp.dot(mat01, piece, preferred_element_type=F32) for piece in _split3(x))


def _dot01_right(x, mat01):
    return sum(jnp.dot(piece, mat01, preferred_element_type=F32) for piece in _split3(x))


def _strips(n_rows, body):
    def step(i, carry):
        body(pl.ds(pl.multiple_of(i * STRIP, STRIP), STRIP))
        return carry
    lax.fori_loop(0, n_rows // STRIP, step, 0)


def _inproj_kernel(x_ref, g_ref, w_ref, b_ref, wg_ref, bg_ref, wgt_ref, bgt_ref,
                   z_ref, gc_ref, gr_ref, h_scr):
    @pl.when(pl.program_id(1) == 0)
    def _():
        g = g_ref[...]

        def norm(rows):
            x = x_ref[rows, :]
            ms = jnp.mean(x * x, axis=-1, keepdims=True)
            h_scr[rows, :] = (x * lax.rsqrt(ms + EPS) * g).astype(BF16)
        _strips(TM_IN, norm)
        h = h_scr[...]
        gc_ref[...] = jnp.dot(h, wg_ref[...], preferred_element_type=F32) + bg_ref[...]
        gr_ref[...] = lax.dot_general(wgt_ref[...], h, (((1,), (1,)), ((), ())),
                                      preferred_element_type=F32) + bgt_ref[...]

    z = jnp.dot(h_scr[...], w_ref[...], preferred_element_type=F32) + b_ref[...]
    z_ref[...] = z.astype(z_ref.dtype)


def _inproj(layer, x2d, norm_g, w_z, b_z, w_g, b_g, w_gt, b_gt):
    T = x2d.shape[0]
    grid = (T // TM_IN, Z_COLS // TN_IN)
    return pl.pallas_call(
        _inproj_kernel,
        grid=grid,
        in_specs=[
            pl.BlockSpec((TM_IN, D_MODEL), lambda i, j: (i, 0)),
            pl.BlockSpec((None, 1, D_MODEL), lambda i, j: (layer, 0, 0)),
            pl.BlockSpec((None, D_MODEL, TN_IN), lambda i, j: (layer, 0, j)),
            pl.BlockSpec((None, 1, TN_IN), lambda i, j: (layer, 0, j)),
            pl.BlockSpec((None, D_MODEL, GATE_LANES), lambda i, j: (layer, 0, 0)),
            pl.BlockSpec((None, 1, GATE_LANES), lambda i, j: (layer, 0, 0)),
            pl.BlockSpec((None, GATE_ROWS, D_MODEL), lambda i, j: (layer, 0, 0)),
            pl.BlockSpec((None, GATE_ROWS, 1), lambda i, j: (layer, 0, 0)),
        ],
        out_specs=[
            pl.BlockSpec((TM_IN, TN_IN), lambda i, j: (i, j)),
            pl.BlockSpec((TM_IN, GATE_LANES), lambda i, j: (i, 0)),
            pl.BlockSpec((GATE_ROWS, TM_IN), lambda i, j: (0, i)),
        ],
        out_shape=[
            jax.ShapeDtypeStruct((T, Z_COLS), BF16),
            jax.ShapeDtypeStruct((T, GATE_LANES), F32),
            jax.ShapeDtypeStruct((GATE_ROWS, T), F32),
        ],
        scratch_shapes=[pltpu.VMEM((TM_IN, D_MODEL), BF16)],
        compiler_params=pltpu.CompilerParams(
            dimension_semantics=("arbitrary", "arbitrary"),
            vmem_limit_bytes=VMEM_LIMIT),
        name="inproj",
    )(x2d, norm_g, w_z, b_z, w_g, b_g, w_gt, b_gt)


def _mixer_kernel(zqk_ref, zv_ref, zo_ref, rq_ref, rk_ref, rv_ref, rg_ref,
                  gc_ref, gr_ref, cw_ref, cb_ref, cos_ref, sin_ref,
                  intra_ref, rcol_ref, mg_ref, rgain_ref,
                  ym_ref, yr_ref,
                  conv_scr, q_scr, k_scr, rqs_scr, rks_scr, hm_scr, hr_scr,
                  c_scr, r_scr, m_scr, *, cdec):
    L = CHUNK_L
    t = pl.program_id(1)

    @pl.when(t == 0)
    def _():
        conv_scr[0:8, :] = jnp.zeros((8, 2 * M_QK), F32)
        c_scr[...] = jnp.zeros(c_scr.shape, F32)
        r_scr[...] = jnp.zeros(r_scr.shape, F32)
        m_scr[...] = jnp.zeros(m_scr.shape, F32)

    conv_scr[8:8 + L, :] = zqk_ref[...].astype(F32)
    cw = cw_ref[...]
    cb = cb_ref[...]

    def conv(rows):
        win = conv_scr[pl.ds(rows.start, STRIP + 8), :]
        acc = cb
        for j in range(CONV_K):
            acc = acc + win[5 + j:5 + j + STRIP, :] * cw[j:j + 1, :]
        y = acc * _sigmoid(acc)
        q_scr[rows, :] = (y[:, :M_QK] * (M_DK ** -0.5)).astype(BF16)
        k_scr[rows, :] = y[:, M_QK:].astype(BF16)
    _strips(L, conv)
    conv_scr[0:8, :] = conv_scr[L:L + 8, :]

    def rope(rows):
        c = cos_ref[rows, :]
        s = sin_ref[rows, :]
        for src, dst, scale in ((rq_ref, rqs_scr, 1.0), (rk_ref, rks_scr, R_DK ** -0.5)):
            for h in range(HEADS):
                lo = h * R_DK
                x1 = src[rows, lo:lo + LANE].astype(F32)
                x2 = src[rows, lo + LANE:lo + 2 * LANE].astype(F32)
                dst[rows, lo:lo + LANE] = ((x1 * c - x2 * s) * scale).astype(BF16)
                dst[rows, lo + LANE:lo + 2 * LANE] = ((x1 * s + x2 * c) * scale).astype(BF16)
    _strips(L, rope)

    row_id = lax.broadcasted_iota(jnp.int32, (L, L), 0)
    col_id = lax.broadcasted_iota(jnp.int32, (L, L), 1)
    causal = row_id >= col_id
    tril = causal.astype(BF16)
    triu = (row_id <= col_id).astype(BF16)
    gcol = gc_ref[...]
    grow = gr_ref[...]
    b_cols = _dot01_left(tril, _log_sigmoid(gcol))
    b_rows = _dot01_right(_log_sigmoid(grow), triu)
    ones_pad = jnp.ones((L, LANE), BF16)
    rcol = rcol_ref[...]

    for h in range(HEADS):
        sl = slice(h * M_DK, (h + 1) * M_DK)
        q = q_scr[:, sl]
        k = k_scr[:, sl]
        vaug = jnp.concatenate([zv_ref[:, sl], ones_pad], axis=1)
        bcol = b_cols[:, HEADS + h:HEADS + h + 1]
        licol = gcol[:, h:h + 1]
        brow = b_rows[HEADS + h:HEADS + h + 1, :]
        lirow = grow[h:h + 1, :]
        m_prev = m_scr[h][0:1, 0:1]

        log_intra = jnp.where(causal, bcol - brow + lirow, -jnp.inf)
        inter = bcol + m_prev
        m_t = jnp.maximum(inter, jnp.max(log_intra, axis=-1, keepdims=True))
        dmat = jnp.exp(log_intra - m_t)
        d_inter = jnp.exp(inter - m_t)
        s = lax.dot_general(q, k, (((1,), (1,)), ((), ())), preferred_element_type=F32) * dmat
        c_old = c_scr[h]
        num = (jnp.dot(s.astype(BF16), vaug, preferred_element_type=F32)
               + d_inter * jnp.dot(q, c_old.astype(BF16), preferred_element_type=F32))
        den = num[:, M_DV:M_DV + 1]
        scale = 1.0 / jnp.maximum(jnp.abs(den), jnp.exp(-m_t))
        hm_scr[:, sl] = num[:, :M_DV] * scale

        b_end = bcol[L - 1:L, :]
        log_w = b_end - bcol + licol
        m_new = jnp.maximum(b_end + m_prev, jnp.max(log_w, axis=0, keepdims=True))
        w = jnp.exp(log_w - m_new)
        decay = jnp.exp(b_end + m_prev - m_new)
        kw_t = (k.astype(F32) * w).T.astype(BF16)
        c_scr[h] = decay * c_old + jnp.dot(kw_t, vaug, preferred_element_type=F32)
        m_scr[h] = jnp.broadcast_to(m_new, m_scr.shape[1:])

        sl = slice(h * R_DK, (h + 1) * R_DK)
        vs = slice(h * R_DV, (h + 1) * R_DV)
        rq = rqs_scr[:, sl]
        rk = rks_scr[:, sl]
        rv = rv_ref[:, vs]
        s = lax.dot_general(rq, rk, (((1,), (1,)), ((), ())), preferred_element_type=F32) * intra_ref[h]
        r_old = r_scr[h]
        hr_scr[:, vs] = (jnp.dot(s.astype(BF16), rv, preferred_element_type=F32)
                         + jnp.dot(rq, r_old.astype(BF16), preferred_element_type=F32)
                         * rcol[:, h:h + 1])
        rkw_t = (rk.astype(F32) * rcol[:, HEADS + h:HEADS + h + 1]).T.astype(BF16)
        r_scr[h] = cdec[h] * r_old + jnp.dot(rkw_t, rv, preferred_element_type=F32)

    mg = mg_ref[...]
    rgain = rgain_ref[...]

    def head_norm(x):
        mu = jnp.mean(x, axis=-1, keepdims=True)
        xc = x - mu
        var = jnp.mean(xc * xc, axis=-1, keepdims=True)
        return xc * lax.rsqrt(var + EPS)

    def finish(rows):
        for h in range(HEADS):
            sl = slice(h * M_DV, (h + 1) * M_DV)
            y = _sigmoid(zo_ref[rows, sl].astype(F32)) * (head_norm(hm_scr[rows, sl]) * mg[:, sl])
            ym_ref[rows, sl] = y.astype(BF16)
            vs = slice(h * R_DV, (h + 1) * R_DV)
            g = rg_ref[rows, vs].astype(F32)
            y = (g * _sigmoid(g)) * (head_norm(hr_scr[rows, vs]) * rgain[:, vs])
            yr_ref[rows, vs] = y.astype(BF16)
    _strips(L, finish)


def _retention_tables(L):
    gamma = 1.0 - np.exp2(-5.0 - np.arange(HEADS, dtype=np.float64))
    lg = np.log(gamma)
    pos = np.arange(L, dtype=np.float64)
    diff = pos[:, None] - pos[None, :]
    intra = np.where(diff >= 0, np.exp(np.maximum(diff, 0.0) * lg[:, None, None]), 0.0)
    rcol = np.zeros((L, LANE), np.float64)
    rcol[:, :HEADS] = np.exp((pos[:, None] + 1.0) * lg[None, :])
    rcol[:, HEADS:2 * HEADS] = np.exp((L - 1.0 - pos[:, None]) * lg[None, :])
    cdec = tuple(float(v) for v in np.exp(L * lg))
    return jnp.asarray(intra, F32), jnp.asarray(rcol, F32), cdec


def _mixer(layer, batch, seq, z, gcol, grow, conv_w, conv_b, cos, sin, m_gain, r_gain):
    L = CHUNK_L
    nt = seq // L
    intra, rcol, cdec = _retention_tables(L)

    def zspec(width, col0):
        blk = col0 // width
        return pl.BlockSpec((L, width), lambda b, t: (b * nt + t, blk))

    def const(shape):
        return pl.BlockSpec(shape, lambda b, t: tuple(0 for _ in shape))

    def per_layer(width):
        return pl.BlockSpec((None, 1, width), lambda b, t: (layer, 0, 0))

    T = batch * seq
    return pl.pallas_call(
        functools.partial(_mixer_kernel, cdec=cdec),
        grid=(batch, nt),
        in_specs=[
            zspec(2 * M_QK, Z_MQ), zspec(M_V, Z_MV), zspec(M_V, Z_MO),
            zspec(R_QK, Z_RQ), zspec(R_QK, Z_RK), zspec(R_V, Z_RV), zspec(R_V, Z_RG),
            pl.BlockSpec((L, GATE_LANES), lambda b, t: (b * nt + t, 0)),
            pl.BlockSpec((GATE_ROWS, L), lambda b, t: (0, b * nt + t)),
            pl.BlockSpec((None, CONV_K, 2 * M_QK), lambda b, t: (layer, 0, 0)),
            per_layer(2 * M_QK),
            pl.BlockSpec((L, LANE), lambda b, t: (t, 0)),
            pl.BlockSpec((L, LANE), lambda b, t: (t, 0)),
            const((HEADS, L, L)), const((L, LANE)),
            per_layer(M_V), per_layer(R_V),
        ],
        out_specs=[
            pl.BlockSpec((L, M_V), lambda b, t: (b * nt + t, 0)),
            pl.BlockSpec((L, R_V), lambda b, t: (b * nt + t, 0)),
        ],
        out_shape=[
            jax.ShapeDtypeStruct((T, M_V), BF16),
            jax.ShapeDtypeStruct((T, R_V), BF16),
        ],
        scratch_shapes=[
            pltpu.VMEM((L + 8, 2 * M_QK), F32),
            pltpu.VMEM((L, M_QK), BF16), pltpu.VMEM((L, M_QK), BF16),
            pltpu.VMEM((L, R_QK), BF16), pltpu.VMEM((L, R_QK), BF16),
            pltpu.VMEM((L, M_V), F32), pltpu.VMEM((L, R_V), F32),
            pltpu.VMEM((HEADS, M_DK, M_DV + LANE), F32),
            pltpu.VMEM((HEADS, R_DK, R_DV), F32),
            pltpu.VMEM((HEADS, 8, LANE), F32),
        ],
        compiler_params=pltpu.CompilerParams(
            dimension_semantics=("arbitrary", "arbitrary"),
            vmem_limit_bytes=VMEM_LIMIT),
        name="mixer",
    )(z, z, z, z, z, z, z, gcol, grow, conv_w, conv_b, cos, sin, intra, rcol, m_gain, r_gain)


def _rmsnorm_rows(src_ref, dst_ref, g, n_rows):
    def norm(rows):
        x = src_ref[rows, :]
        ms = jnp.mean(x * x, axis=-1, keepdims=True)
        dst_ref[rows, :] = (x * lax.rsqrt(ms + EPS) * g).astype(dst_ref.dtype)
    _strips(n_rows, norm)


def _post_kernel(ym_ref, yr_ref, gm_ref, gr_ref, x_ref, p_ref,
                 wbm_ref, wbr_ref, wout_ref, g2_ref, w1_ref, b1_ref, w2_ref, b2_ref,
                 g3_ref, wpg_ref, wpe_ref, gf_ref,
                 out_ref, x_scr, h_scr, *, final):
    a = jnp.dot(ym_ref[...], wbm_ref[...], preferred_element_type=F32)
    merged = _sigmoid(gm_ref[...].astype(F32)) * a
    b = jnp.dot(yr_ref[...], wbr_ref[...], preferred_element_type=F32)
    merged = merged + _sigmoid(gr_ref[...].astype(F32)) * b
    x_scr[...] = x_ref[...] + jnp.dot(merged.astype(BF16), wout_ref[...], preferred_element_type=F32)

    _rmsnorm_rows(x_scr, h_scr, g2_ref[...], TM_POST)
    h = h_scr[...]
    acc = x_scr[...] + b2_ref[...]
    for c in range(D_FF // D_MODEL):
        cs = slice(c * D_MODEL, (c + 1) * D_MODEL)
        u = jnp.maximum(jnp.dot(h, w1_ref[:, cs], preferred_element_type=F32) + b1_ref[:, cs], 0.0)
        acc = acc + jnp.dot((u * u).astype(BF16), w2_ref[cs, :], preferred_element_type=F32)
    x_scr[...] = acc

    _rmsnorm_rows(x_scr, h_scr, g3_ref[...], TM_POST)
    gate = _sigmoid(jnp.dot(h_scr[...], wpg_ref[...], preferred_element_type=F32))
    pe = jnp.dot(p_ref[...].astype(BF16), wpe_ref[...], preferred_element_type=F32)
    if final:
        x_scr[...] = x_scr[...] + gate * pe
        _rmsnorm_rows(x_scr, out_ref, gf_ref[...], TM_POST)
    else:
        out_ref[...] = x_scr[...] + gate * pe


def _post(layer, final, ym, yr, z, x2d, p3d, w_bm, w_br, w_out, g2, w1, b1, w2, b2, g3, wpg, wpe, gf):
    T = x2d.shape[0]
    tm = TM_POST

    def rows(width, blk=0):
        return pl.BlockSpec((tm, width), lambda i: (i, blk))

    def weight(k, n):
        return pl.BlockSpec((None, k, n), lambda i: (layer, 0, 0), pipeline_mode=pl.Buffered(1))

    def vec(n):
        return pl.BlockSpec((None, 1, n), lambda i: (layer, 0, 0))

    return pl.pallas_call(
        functools.partial(_post_kernel, final=final),
        grid=(T // tm,),
        in_specs=[
            rows(M_V), rows(R_V), rows(D_MODEL, Z_GM // D_MODEL), rows(D_MODEL, Z_GR // D_MODEL),
            rows(D_MODEL),
            pl.BlockSpec((None, tm, PE_DIM), lambda i: (layer, i, 0)),
            weight(M_V, D_MODEL), weight(R_V, D_MODEL), weight(D_MODEL, D_MODEL),
            vec(D_MODEL), weight(D_MODEL, D_FF), vec(D_FF), weight(D_FF, D_MODEL), vec(D_MODEL),
            vec(D_MODEL), weight(D_MODEL, D_MODEL), weight(PE_DIM, D_MODEL),
            pl.BlockSpec((1, D_MODEL), lambda i: (0, 0)),
        ],
        out_specs=rows(D_MODEL),
        out_shape=jax.ShapeDtypeStruct((T, D_MODEL), F32),
        scratch_shapes=[pltpu.VMEM((tm, D_MODEL), F32), pltpu.VMEM((tm, D_MODEL), BF16)],
        compiler_params=pltpu.CompilerParams(
            dimension_semantics=("arbitrary",),
            vmem_limit_bytes=VMEM_LIMIT),
        name="post_final" if final else "post",
    )(ym, yr, z, z, x2d, p3d, w_bm, w_br, w_out, g2, w1, b1, w2, b2, g3, wpg, wpe, gf)


def _reorder_in_proj(w_in, b_in):
    big = [(0, 4096), (4104, 4104 + 6144 + 2048)]
    w_z = jnp.concatenate([w_in[..., a:b] for a, b in big], axis=-1)
    b_z = jnp.concatenate([b_in[..., a:b] for a, b in big], axis=-1)
    w_gate = w_in[..., 4096:4104]
    b_gate = b_in[..., 4096:4104]
    return w_z, b_z, w_gate, b_gate


def kernel(x, p, norm1_g, w_in, b_in, conv_w, conv_b, m_norm_g, r_norm_g, w_bm, w_br, w_out,
           norm2_g, w_ff1, b_ff1, w_ff2, b_ff2, norm3_g, w_pe_gate, w_pe, final_g):
    B, S, _ = x.shape
    depth = w_in.shape[0]
    T = B * S
    assert S % CHUNK_L == 0 and T % TM_IN == 0 and T % TM_POST == 0

    w_z, b_z, w_gate, b_gate = _reorder_in_proj(w_in, b_in)
    w_z = w_z.astype(BF16)
    b_z = b_z[:, None, :]
    pad = GATE_LANES - GATE_ROWS
    w_g = jnp.pad(w_gate, ((0, 0), (0, 0), (0, pad))).astype(BF16)
    b_g = jnp.pad(b_gate, ((0, 0), (0, pad)))[:, None, :]
    w_gt = jnp.swapaxes(w_gate, 1, 2).astype(BF16)
    b_gt = b_gate[:, :, None]

    pos = jnp.arange(S, dtype=F32)
    inv_freq = ROPE_BASE ** (-jnp.arange(0, R_DK, 2, dtype=F32) / R_DK)
    ang = pos[:, None] * inv_freq[None, :]
    cos = jnp.cos(ang)
    sin = jnp.sin(ang)

    bf = lambda w: w.astype(BF16)
    row = lambda v: v[:, None, :]
    w_bm, w_br, w_out, w_ff1, w_ff2, w_pe_gate, w_pe = map(
        bf, (w_bm, w_br, w_out, w_ff1, w_ff2, w_pe_gate, w_pe))
    norm1_g, conv_b, m_norm_g, r_norm_g, norm2_g, b_ff1, b_ff2, norm3_g = map(
        row, (norm1_g, conv_b, m_norm_g, r_norm_g, norm2_g, b_ff1, b_ff2, norm3_g))
    final_g = final_g[None, :]

    x2d = x.reshape(T, D_MODEL)
    p3d = p.reshape(depth, T, PE_DIM)
    for i in range(depth):
        z, gcol, grow = _inproj(i, x2d, norm1_g, w_z, b_z, w_g, b_g, w_gt, b_gt)
        ym, yr = _mixer(i, B, S, z, gcol, grow, conv_w, conv_b, cos, sin, m_norm_g, r_norm_g)
        x2d = _post(i, i == depth - 1, ym, yr, z, x2d, p3d, w_bm, w_br, w_out,
                    norm2_g, w_ff1, b_ff1, w_ff2, b_ff2, norm3_g, w_pe_gate, w_pe, final_g)
    return x2d.reshape(B, S, D_MODEL)
```

```python
import functools

import numpy as np
import jax
import jax.numpy as jnp
from jax import lax
from jax.experimental import pallas as pl
from jax.experimental.pallas import tpu as pltpu

F32 = jnp.float32
BF16 = jnp.bfloat16

D_MODEL = 1024
PE_DIM = 256
EPS = 1e-6
HEADS = 4
M_DK = 256
M_DV = 256
R_DK = 256
R_DV = 512
M_QK = HEADS * M_DK
M_V = HEADS * M_DV
R_QK = HEADS * R_DK
R_V = HEADS * R_DV
CONV_K = 4
ROPE_BASE = 10000.0
D_FF = 4 * D_MODEL

Z_MQ, Z_MK, Z_MV, Z_MO = 0, 1024, 2048, 3072
Z_RQ, Z_RK, Z_RV, Z_RG = 4096, 5120, 6144, 8192
Z_GM, Z_GR = 10240, 11264
Z_COLS = 12288
GATE_LANES = 128
GATE_ROWS = 8

LANE = 128
VMEM_LIMIT = 56 * 1024 * 1024

TM_IN = 2048
TN_IN = 1024
CHUNK_L = 256
TM_POST = 256
STRIP = 16
CONV_ROWS = 32
CONV_COLS = 512


def _sigmoid(x):
    return 1.0 / (1.0 + jnp.exp(-x))


def _log_sigmoid(x):
    return jnp.minimum(x, 0.0) - jnp.log1p(jnp.exp(-jnp.abs(x)))


def _split3(x):
    hi = x.astype(BF16)
    r1 = x - hi.astype(F32)
    mid = r1.astype(BF16)
    lo = (r1 - mid.astype(F32)).astype(BF16)
    return hi, mid, lo


def _dot01_left(mat01, x):
    return sum(jnp.dot(mat01, piece, preferred_element_type=F32) for piece in _split3(x))


def _dot01_right(x, mat01):
    return sum(jnp.dot(piece, mat01, preferred_element_type=F32) for piece in _split3(x))


def _strips(n_rows, body, rows_per_strip=STRIP, unroll=1):
    def step(i, carry):
        body(pl.ds(pl.multiple_of(i * rows_per_strip, rows_per_strip), rows_per_strip))
        return carry
    lax.fori_loop(0, n_rows // rows_per_strip, step, 0, unroll=unroll)


def _inproj_kernel(x_ref, g_ref, w_ref, b_ref, wg_ref, bg_ref, wgt_ref, bgt_ref,
                   z_ref, gc_ref, gr_ref, h_scr):
    @pl.when(pl.program_id(1) == 0)
    def _():
        g = g_ref[...]

        def norm(rows):
            x = x_ref[rows, :]
            ms = jnp.mean(x * x, axis=-1, keepdims=True)
            h_scr[rows, :] = (x * lax.rsqrt(ms + EPS) * g).astype(BF16)
        _strips(TM_IN, norm, unroll=8)
        h = h_scr[...]
        gc_ref[...] = jnp.dot(h, wg_ref[...], preferred_element_type=F32) + bg_ref[...]
        gr_ref[...] = lax.dot_general(wgt_ref[...], h, (((1,), (1,)), ((), ())),
                                      preferred_element_type=F32) + bgt_ref[...]

    z = jnp.dot(h_scr[...], w_ref[...], preferred_element_type=F32) + b_ref[...]
    z_ref[...] = z.astype(z_ref.dtype)


def _inproj(layer, x2d, norm_g, w_z, b_z, w_g, b_g, w_gt, b_gt):
    T = x2d.shape[0]
    grid = (T // TM_IN, Z_COLS // TN_IN)
    return pl.pallas_call(
        _inproj_kernel,
        grid=grid,
        in_specs=[
            pl.BlockSpec((TM_IN, D_MODEL), lambda i, j: (i, 0)),
            pl.BlockSpec((None, 1, D_MODEL), lambda i, j: (layer, 0, 0)),
            pl.BlockSpec((None, D_MODEL, TN_IN), lambda i, j: (layer, 0, j)),
            pl.BlockSpec((None, 1, TN_IN), lambda i, j: (layer, 0, j)),
            pl.BlockSpec((None, D_MODEL, GATE_LANES), lambda i, j: (layer, 0, 0)),
            pl.BlockSpec((None, 1, GATE_LANES), lambda i, j: (layer, 0, 0)),
            pl.BlockSpec((None, GATE_ROWS, D_MODEL), lambda i, j: (layer, 0, 0)),
            pl.BlockSpec((None, GATE_ROWS, 1), lambda i, j: (layer, 0, 0)),
        ],
        out_specs=[
            pl.BlockSpec((TM_IN, TN_IN), lambda i, j: (i, j)),
            pl.BlockSpec((TM_IN, GATE_LANES), lambda i, j: (i, 0)),
            pl.BlockSpec((GATE_ROWS, TM_IN), lambda i, j: (0, i)),
        ],
        out_shape=[
            jax.ShapeDtypeStruct((T, Z_COLS), BF16),
            jax.ShapeDtypeStruct((T, GATE_LANES), F32),
            jax.ShapeDtypeStruct((GATE_ROWS, T), F32),
        ],
        scratch_shapes=[pltpu.VMEM((TM_IN, D_MODEL), BF16)],
        compiler_params=pltpu.CompilerParams(
            dimension_semantics=("arbitrary", "arbitrary"),
            vmem_limit_bytes=VMEM_LIMIT),
        name="inproj",
    )(x2d, norm_g, w_z, b_z, w_g, b_g, w_gt, b_gt)


def _mixer_kernel(zqk_ref, zv_ref, zo_ref, rq_ref, rk_ref, rv_ref, rg_ref,
                  gc_ref, gr_ref, cw_ref, cb_ref, cos_ref, sin_ref,
                  intra_ref, rcol_ref, mg_ref, rgain_ref,
                  ym_ref, yr_ref,
                  conv_scr, q_scr, k_scr, rqs_scr, rks_scr, hm_scr, hr_scr,
                  c_scr, r_scr, m_scr, *, cdec):
    L = CHUNK_L
    t = pl.program_id(1)

    @pl.when(t == 0)
    def _():
        conv_scr[0:8, :] = jnp.zeros((8, 2 * M_QK), F32)
        c_scr[...] = jnp.zeros(c_scr.shape, F32)
        r_scr[...] = jnp.zeros(r_scr.shape, F32)
        m_scr[...] = jnp.zeros(m_scr.shape, F32)

    conv_scr[8:8 + L, :] = zqk_ref[...].astype(F32)
    cw = cw_ref[...]
    cb = cb_ref[...]

    def conv(rows):
        for c0 in range(0, 2 * M_QK, CONV_COLS):
            cols = slice(c0, c0 + CONV_COLS)
            win = conv_scr[pl.ds(rows.start, CONV_ROWS + 8), cols]
            acc = win * cw[0:1, cols]
            for j in range(1, CONV_K):
                acc = win * cw[j:j + 1, cols] + pltpu.roll(acc, 1, axis=0)
            y = acc[8:, :] + cb[:, cols]
            y = y * _sigmoid(y)
            if c0 < M_QK:
                q_scr[rows, cols] = (y * (M_DK ** -0.5)).astype(BF16)
            else:
                k_scr[rows, slice(c0 - M_QK, c0 - M_QK + CONV_COLS)] = y.astype(BF16)
    _strips(L, conv, rows_per_strip=CONV_ROWS)
    conv_scr[0:8, :] = conv_scr[L:L + 8, :]

    def rope(rows):
        c = cos_ref[rows, :]
        s = sin_ref[rows, :]
        for src, dst, scale in ((rq_ref, rqs_scr, 1.0), (rk_ref, rks_scr, R_DK ** -0.5)):
            for h in range(HEADS):
                lo = h * R_DK
                x1 = src[rows, lo:lo + LANE].astype(F32)
                x2 = src[rows, lo + LANE:lo + 2 * LANE].astype(F32)
                dst[rows, lo:lo + LANE] = ((x1 * c - x2 * s) * scale).astype(BF16)
                dst[rows, lo + LANE:lo + 2 * LANE] = ((x1 * s + x2 * c) * scale).astype(BF16)
    _strips(L, rope, unroll=2)

    row_id = lax.broadcasted_iota(jnp.int32, (L, L), 0)
    col_id = lax.broadcasted_iota(jnp.int32, (L, L), 1)
    causal = row_id >= col_id
    tril = causal.astype(BF16)
    triu = (row_id <= col_id).astype(BF16)
    gcol = gc_ref[...]
    grow = gr_ref[...]
    b_cols = _dot01_left(tril, _log_sigmoid(gcol))
    b_rows = _dot01_right(_log_sigmoid(grow), triu)
    ones_pad = jnp.ones((L, LANE), BF16)
    rcol = rcol_ref[...]

    for h in range(HEADS):
        sl = slice(h * M_DK, (h + 1) * M_DK)
        q = q_scr[:, sl]
        k = k_scr[:, sl]
        vaug = jnp.concatenate([zv_ref[:, sl], ones_pad], axis=1)
        bcol = b_cols[:, HEADS + h:HEADS + h + 1]
        licol = gcol[:, h:h + 1]
        brow = b_rows[HEADS + h:HEADS + h + 1, :]
        lirow = grow[h:h + 1, :]
        m_prev = m_scr[h][0:1, 0:1]

        log_intra = jnp.where(causal, bcol - brow + lirow, -jnp.inf)
        inter = bcol + m_prev
        m_t = jnp.maximum(inter, jnp.max(log_intra, axis=-1, keepdims=True))
        dmat = jnp.exp(log_intra - m_t)
        d_inter = jnp.exp(inter - m_t)
        s = lax.dot_general(q, k, (((1,), (1,)), ((), ())), preferred_element_type=F32) * dmat
        c_old = c_scr[h]
        num = (jnp.dot(s.astype(BF16), vaug, preferred_element_type=F32)
               + d_inter * jnp.dot(q, c_old.astype(BF16), preferred_element_type=F32))
        den = num[:, M_DV:M_DV + 1]
        scale = 1.0 / jnp.maximum(jnp.abs(den), jnp.exp(-m_t))
        hm_scr[:, sl] = num[:, :M_DV] * scale

        b_end = bcol[L - 1:L, :]
        log_w = b_end - bcol + licol
        m_new = jnp.maximum(b_end + m_prev, jnp.max(log_w, axis=0, keepdims=True))
        w = jnp.exp(log_w - m_new)
        decay = jnp.exp(b_end + m_prev - m_new)
        kw_t = (k.astype(F32) * w).T.astype(BF16)
        c_scr[h] = decay * c_old + jnp.dot(kw_t, vaug, preferred_element_type=F32)
        m_scr[h] = jnp.broadcast_to(m_new, m_scr.shape[1:])

        sl = slice(h * R_DK, (h + 1) * R_DK)
        vs = slice(h * R_DV, (h + 1) * R_DV)
        rq = rqs_scr[:, sl]
        rk = rks_scr[:, sl]
        rv = rv_ref[:, vs]
        s = lax.dot_general(rq, rk, (((1,), (1,)), ((), ())), preferred_element_type=F32) * intra_ref[h]
        r_old = r_scr[h]
        hr_scr[:, vs] = (jnp.dot(s.astype(BF16), rv, preferred_element_type=F32)
                         + jnp.dot(rq, r_old.astype(BF16), preferred_element_type=F32)
                         * rcol[:, h:h + 1])
        rkw_t = (rk.astype(F32) * rcol[:, HEADS + h:HEADS + h + 1]).T.astype(BF16)
        r_scr[h] = cdec[h] * r_old + jnp.dot(rkw_t, rv, preferred_element_type=F32)

    mg = mg_ref[...]
    rgain = rgain_ref[...]

    def head_norm(x):
        mu = jnp.mean(x, axis=-1, keepdims=True)
        xc = x - mu
        var = jnp.mean(xc * xc, axis=-1, keepdims=True)
        return xc * lax.rsqrt(var + EPS)

    def finish(rows):
        for h in range(HEADS):
            sl = slice(h * M_DV, (h + 1) * M_DV)
            y = _sigmoid(zo_ref[rows, sl].astype(F32)) * (head_norm(hm_scr[rows, sl]) * mg[:, sl])
            ym_ref[rows, sl] = y.astype(BF16)
            vs = slice(h * R_DV, (h + 1) * R_DV)
            g = rg_ref[rows, vs].astype(F32)
            y = (g * _sigmoid(g)) * (head_norm(hr_scr[rows, vs]) * rgain[:, vs])
            yr_ref[rows, vs] = y.astype(BF16)
    _strips(L, finish, unroll=2)


def _retention_tables(L):
    gamma = 1.0 - np.exp2(-5.0 - np.arange(HEADS, dtype=np.float64))
    lg = np.log(gamma)
    pos = np.arange(L, dtype=np.float64)
    diff = pos[:, None] - pos[None, :]
    intra = np.where(diff >= 0, np.exp(np.maximum(diff, 0.0) * lg[:, None, None]), 0.0)
    rcol = np.zeros((L, LANE), np.float64)
    rcol[:, :HEADS] = np.exp((pos[:, None] + 1.0) * lg[None, :])
    rcol[:, HEADS:2 * HEADS] = np.exp((L - 1.0 - pos[:, None]) * lg[None, :])
    cdec = tuple(float(v) for v in np.exp(L * lg))
    return jnp.asarray(intra, F32), jnp.asarray(rcol, F32), cdec


def _mixer(layer, batch, seq, z, gcol, grow, conv_w, conv_b, cos, sin, m_gain, r_gain):
    L = CHUNK_L
    nt = seq // L
    intra, rcol, cdec = _retention_tables(L)

    def zspec(width, col0):
        blk = col0 // width
        return pl.BlockSpec((L, width), lambda b, t: (b * nt + t, blk))

    def const(shape):
        return pl.BlockSpec(shape, lambda b, t: tuple(0 for _ in shape))

    def per_layer(width):
        return pl.BlockSpec((None, 1, width), lambda b, t: (layer, 0, 0))

    T = batch * seq
    return pl.pallas_call(
        functools.partial(_mixer_kernel, cdec=cdec),
        grid=(batch, nt),
        in_specs=[
            zspec(2 * M_QK, Z_MQ), zspec(M_V, Z_MV), zspec(M_V, Z_MO),
            zspec(R_QK, Z_RQ), zspec(R_QK, Z_RK), zspec(R_V, Z_RV), zspec(R_V, Z_RG),
            pl.BlockSpec((L, GATE_LANES), lambda b, t: (b * nt + t, 0)),
            pl.BlockSpec((GATE_ROWS, L), lambda b, t: (0, b * nt + t)),
            pl.BlockSpec((None, CONV_K, 2 * M_QK), lambda b, t: (layer, 0, 0)),
            per_layer(2 * M_QK),
            pl.BlockSpec((L, LANE), lambda b, t: (t, 0)),
            pl.BlockSpec((L, LANE), lambda b, t: (t, 0)),
            const((HEADS, L, L)), const((L, LANE)),
            per_layer(M_V), per_layer(R_V),
        ],
        out_specs=[
            pl.BlockSpec((L, M_V), lambda b, t: (b * nt + t, 0)),
            pl.BlockSpec((L, R_V), lambda b, t: (b * nt + t, 0)),
        ],
        out_shape=[
            jax.ShapeDtypeStruct((T, M_V), BF16),
            jax.ShapeDtypeStruct((T, R_V), BF16),
        ],
        scratch_shapes=[
            pltpu.VMEM((L + 8, 2 * M_QK), F32),
            pltpu.VMEM((L, M_QK), BF16), pltpu.VMEM((L, M_QK), BF16),
            pltpu.VMEM((L, R_QK), BF16), pltpu.VMEM((L, R_QK), BF16),
            pltpu.VMEM((L, M_V), F32), pltpu.VMEM((L, R_V), F32),
            pltpu.VMEM((HEADS, M_DK, M_DV + LANE), F32),
            pltpu.VMEM((HEADS, R_DK, R_DV), F32),
            pltpu.VMEM((HEADS, 8, LANE), F32),
        ],
        compiler_params=pltpu.CompilerParams(
            dimension_semantics=("arbitrary", "arbitrary"),
            vmem_limit_bytes=VMEM_LIMIT),
        name="mixer",
    )(z, z, z, z, z, z, z, gcol, grow, conv_w, conv_b, cos, sin, intra, rcol, m_gain, r_gain)


def _rmsnorm_rows(src_ref, dst_ref, g, n_rows):
    def norm(rows):
        x = src_ref[rows, :]
        ms = jnp.mean(x * x, axis=-1, keepdims=True)
        dst_ref[rows, :] = (x * lax.rsqrt(ms + EPS) * g).astype(dst_ref.dtype)
    _strips(n_rows, norm, unroll=8)


def _post_kernel(ym_ref, yr_ref, gm_ref, gr_ref, x_ref, p_ref,
                 wbm_ref, wbr_ref, wout_ref, g2_ref, w1_ref, b1_ref, w2_ref, b2_ref,
                 g3_ref, wpg_ref, wpe_ref, gf_ref,
                 out_ref, x_scr, h_scr, *, final):
    a = jnp.dot(ym_ref[...], wbm_ref[...], preferred_element_type=F32)
    merged = _sigmoid(gm_ref[...].astype(F32)) * a
    b = jnp.dot(yr_ref[...], wbr_ref[...], preferred_element_type=F32)
    merged = merged + _sigmoid(gr_ref[...].astype(F32)) * b
    x_scr[...] = x_ref[...] + jnp.dot(merged.astype(BF16), wout_ref[...], preferred_element_type=F32)

    _rmsnorm_rows(x_scr, h_scr, g2_ref[...], TM_POST)
    h = h_scr[...]
    acc = x_scr[...] + b2_ref[...]
    for c in range(D_FF // D_MODEL):
        cs = slice(c * D_MODEL, (c + 1) * D_MODEL)
        u = jnp.maximum(jnp.dot(h, w1_ref[:, cs], preferred_element_type=F32) + b1_ref[:, cs], 0.0)
        acc = acc + jnp.dot((u * u).astype(BF16), w2_ref[cs, :], preferred_element_type=F32)
    x_scr[...] = acc

    _rmsnorm_rows(x_scr, h_scr, g3_ref[...], TM_POST)
    gate = _sigmoid(jnp.dot(h_scr[...], wpg_ref[...], preferred_element_type=F32))
    pe = jnp.dot(p_ref[...].astype(BF16), wpe_ref[...], preferred_element_type=F32)
    if final:
        x_scr[...] = x_scr[...] + gate * pe
        _rmsnorm_rows(x_scr, out_ref, gf_ref[...], TM_POST)
    else:
        out_ref[...] = x_scr[...] + gate * pe


def _post(layer, final, ym, yr, z, x2d, p3d, w_bm, w_br, w_out, g2, w1, b1, w2, b2, g3, wpg, wpe, gf):
    T = x2d.shape[0]
    tm = TM_POST

    def rows(width, blk=0):
        return pl.BlockSpec((tm, width), lambda i: (i, blk))

    def weight(k, n):
        return pl.BlockSpec((None, k, n), lambda i: (layer, 0, 0), pipeline_mode=pl.Buffered(1))

    def vec(n):
        return pl.BlockSpec((None, 1, n), lambda i: (layer, 0, 0))

    return pl.pallas_call(
        functools.partial(_post_kernel, final=final),
        grid=(T // tm,),
        in_specs=[
            rows(M_V), rows(R_V), rows(D_MODEL, Z_GM // D_MODEL), rows(D_MODEL, Z_GR // D_MODEL),
            rows(D_MODEL),
            pl.BlockSpec((None, tm, PE_DIM), lambda i: (layer, i, 0)),
            weight(M_V, D_MODEL), weight(R_V, D_MODEL), weight(D_MODEL, D_MODEL),
            vec(D_MODEL), weight(D_MODEL, D_FF), vec(D_FF), weight(D_FF, D_MODEL), vec(D_MODEL),
            vec(D_MODEL), weight(D_MODEL, D_MODEL), weight(PE_DIM, D_MODEL),
            pl.BlockSpec((1, D_MODEL), lambda i: (0, 0)),
        ],
        out_specs=rows(D_MODEL),
        out_shape=jax.ShapeDtypeStruct((T, D_MODEL), F32),
        scratch_shapes=[pltpu.VMEM((tm, D_MODEL), F32), pltpu.VMEM((tm, D_MODEL), BF16)],
        compiler_params=pltpu.CompilerParams(
            dimension_semantics=("arbitrary",),
            vmem_limit_bytes=VMEM_LIMIT),
        name="post_final" if final else "post",
    )(ym, yr, z, z, x2d, p3d, w_bm, w_br, w_out, g2, w1, b1, w2, b2, g3, wpg, wpe, gf)


def _reorder_in_proj(w_in, b_in):
    big = [(0, 4096), (4104, 4104 + 6144 + 2048)]
    w_z = jnp.concatenate([w_in[..., a:b] for a, b in big], axis=-1)
    b_z = jnp.concatenate([b_in[..., a:b] for a, b in big], axis=-1)
    w_gate = w_in[..., 4096:4104]
    b_gate = b_in[..., 4096:4104]
    return w_z, b_z, w_gate, b_gate


def kernel(x, p, norm1_g, w_in, b_in, conv_w, conv_b, m_norm_g, r_norm_g, w_bm, w_br, w_out,
           norm2_g, w_ff1, b_ff1, w_ff2, b_ff2, norm3_g, w_pe_gate, w_pe, final_g):
    B, S, _ = x.shape
    depth = w_in.shape[0]
    T = B * S
    assert S % CHUNK_L == 0 and T % TM_IN == 0 and T % TM_POST == 0

    w_z, b_z, w_gate, b_gate = _reorder_in_proj(w_in, b_in)
    w_z = w_z.astype(BF16)
    b_z = b_z[:, None, :]
    pad = GATE_LANES - GATE_ROWS
    w_g = jnp.pad(w_gate, ((0, 0), (0, 0), (0, pad))).astype(BF16)
    b_g = jnp.pad(b_gate, ((0, 0), (0, pad)))[:, None, :]
    w_gt = jnp.swapaxes(w_gate, 1, 2).astype(BF16)
    b_gt = b_gate[:, :, None]

    pos = jnp.arange(S, dtype=F32)
    inv_freq = ROPE_BASE ** (-jnp.arange(0, R_DK, 2, dtype=F32) / R_DK)
    ang = pos[:, None] * inv_freq[None, :]
    cos = jnp.cos(ang)
    sin = jnp.sin(ang)

    bf = lambda w: w.astype(BF16)
    row = lambda v: v[:, None, :]
    w_bm, w_br, w_out, w_ff1, w_ff2, w_pe_gate, w_pe = map(
        bf, (w_bm, w_br, w_out, w_ff1, w_ff2, w_pe_gate, w_pe))
    norm1_g, conv_b, m_norm_g, r_norm_g, norm2_g, b_ff1, b_ff2, norm3_g = map(
        row, (norm1_g, conv_b, m_norm_g, r_norm_g, norm2_g, b_ff1, b_ff2, norm3_g))
    final_g = final_g[None, :]

    x2d = x.reshape(T, D_MODEL)
    p3d = p.reshape(depth, T, PE_DIM)
    for i in range(depth):
        z, gcol, grow = _inproj(i, x2d, norm1_g, w_z, b_z, w_g, b_g, w_gt, b_gt)
        ym, yr = _mixer(i, B, S, z, gcol, grow, conv_w, conv_b, cos, sin, m_norm_g, r_norm_g)
        x2d = _post(i, i == depth - 1, ym, yr, z, x2d, p3d, w_bm, w_br, w_out,
                    norm2_g, w_ff1, b_ff1, w_ff2, b_ff2, norm3_g, w_pe_gate, w_pe, final_g)
    return x2d.reshape(B, S, D_MODEL)
```

```python
import functools

import numpy as np
import jax
import jax.numpy as jnp
from jax import lax
from jax.experimental import pallas as pl
from jax.experimental.pallas import tpu as pltpu

F32 = jnp.float32
BF16 = jnp.bfloat16

D_MODEL = 1024
PE_DIM = 256
EPS = 1e-6
HEADS = 4
M_DK = 256
M_DV = 256
R_DK = 256
R_DV = 512
M_QK = HEADS * M_DK
M_V = HEADS * M_DV
R_QK = HEADS * R_DK
R_V = HEADS * R_DV
CONV_K = 4
ROPE_BASE = 10000.0
D_FF = 4 * D_MODEL

Z_MQ, Z_MK, Z_MV, Z_MO = 0, 1024, 2048, 3072
Z_RQ, Z_RK, Z_RV, Z_RG = 4096, 5120, 6144, 8192
Z_GM, Z_GR = 10240, 11264
Z_COLS = 12288
GATE_LANES = 128
GATE_ROWS = 8

LANE = 128
VMEM_LIMIT = 56 * 1024 * 1024

TM_IN = 2048
TN_IN = 1024
SUB_IN = 256
N_SUB = TM_IN // SUB_IN
EPI_COLS = 256
CONV_COLS = 128
CHUNK_L = 256
TM_POST = 256
STRIP = 16


def _sigmoid(x):
    return 1.0 / (1.0 + jnp.exp(-x))


def _log_sigmoid(x):
    return jnp.minimum(x, 0.0) - jnp.log1p(jnp.exp(-jnp.abs(x)))


def _split3(x):
    hi = x.astype(BF16)
    r1 = x - hi.astype(F32)
    mid = r1.astype(BF16)
    lo = (r1 - mid.astype(F32)).astype(BF16)
    return hi, mid, lo


def _dot01_left(mat01, x):
    return sum(jnp.dot(mat01, piece, preferred_element_type=F32) for piece in _split3(x))


def _dot01_right(x, mat01):
    return sum(jnp.dot(piece, mat01, preferred_element_type=F32) for piece in _split3(x))


def _strips(n_rows, body, rows_per_strip=STRIP, unroll=1):
    def step(i, carry):
        body(pl.ds(pl.multiple_of(i * rows_per_strip, rows_per_strip), rows_per_strip))
        return carry
    lax.fori_loop(0, n_rows // rows_per_strip, step, 0, unroll=unroll)


def _inproj_kernel(x_ref, g_ref, w_ref, b_ref, wg_ref, bg_ref, wgt_ref, bgt_ref,
                   cw_ref, cb_ref, cos_ref, sin_ref,
                   z_ref, gc_ref, gr_ref, h_scr, carry_scr, acc_scr, *, tiles_per_seq):
    i = pl.program_id(0)
    j = pl.program_id(1)

    @pl.when(j == 0)
    def _():
        g = g_ref[...]

        def norm(rows):
            x = x_ref[rows, :]
            ms = jnp.mean(x * x, axis=-1, keepdims=True)
            h_scr[rows, :] = (x * lax.rsqrt(ms + EPS) * g).astype(BF16)
        _strips(TM_IN, norm, unroll=8)
        h = h_scr[...]
        gc_ref[...] = jnp.dot(h, wg_ref[...], preferred_element_type=F32) + bg_ref[...]
        gr_ref[...] = lax.dot_general(wgt_ref[...], h, (((1,), (1,)), ((), ())),
                                      preferred_element_type=F32) + bgt_ref[...]

    def run(epilogue, epi_cols=EPI_COLS):
        def rows_of(r):
            start = r * SUB_IN
            return pl.ds(start if isinstance(r, int) else pl.multiple_of(start, SUB_IN), SUB_IN)

        def matmul(r, slot):
            acc_scr[slot] = (jnp.dot(h_scr[rows_of(r), :], w_ref[...], preferred_element_type=F32)
                             + b_ref[...])

        def finish(r, slot):
            rows = rows_of(r)
            for c0 in range(0, TN_IN, epi_cols):
                cols = slice(c0, c0 + epi_cols)
                z_ref[rows, cols] = epilogue(acc_scr[slot, :, cols], rows, cols).astype(BF16)

        matmul(0, 0)

        def step(it, carry):
            r = 2 * it + 1
            matmul(r, 1)
            finish(r - 1, 0)
            matmul(r + 1, 0)
            finish(r, 1)
            return carry
        lax.fori_loop(0, (N_SUB - 2) // 2, step, 0)
        matmul(N_SUB - 1, 1)
        finish(N_SUB - 2, 0)
        finish(N_SUB - 1, 1)

    is_conv = j < Z_MV // TN_IN
    is_rope = (j == Z_RQ // TN_IN) | (j == Z_RK // TN_IN)
    is_sigmoid = (j == Z_MO // TN_IN) | (j >= Z_GM // TN_IN)
    is_swish = (j >= Z_RG // TN_IN) & (j < Z_GM // TN_IN)
    is_plain = jnp.logical_not(is_conv | is_rope | is_sigmoid | is_swish)

    @pl.when(is_plain)
    def _():
        z = jnp.dot(h_scr[...], w_ref[...], preferred_element_type=F32) + b_ref[...]
        z_ref[...] = z.astype(BF16)

    @pl.when(is_sigmoid)
    def _():
        run(lambda zc, rows, cols: _sigmoid(zc))

    @pl.when(is_swish)
    def _():
        run(lambda zc, rows, cols: zc * _sigmoid(zc))

    @pl.when(is_rope)
    def _():
        scale = jnp.where(j == Z_RK // TN_IN, R_DK ** -0.5, 1.0)

        def rope(zc, rows, cols):
            c = cos_ref[rows, :] * scale
            s = sin_ref[rows, :] * scale
            x1, x2 = zc[:, :LANE], zc[:, LANE:]
            return jnp.concatenate([x1 * c - x2 * s, x1 * s + x2 * c], axis=1)
        run(rope)

    @pl.when(is_conv)
    def _():
        @pl.when(i % tiles_per_seq == 0)
        def _():
            carry_scr[j] = jnp.zeros(carry_scr.shape[1:], F32)
        scale = jnp.where(j == Z_MQ // TN_IN, M_DK ** -0.5, 1.0)

        def conv(zc, rows, cols):
            ext = jnp.concatenate([carry_scr[j, :, cols], zc], axis=0)
            carry_scr[j, :, cols] = zc[SUB_IN - 8:, :]
            acc = ext * cw_ref[0:1, cols]
            for tap in range(1, CONV_K):
                acc = ext * cw_ref[tap:tap + 1, cols] + pltpu.roll(acc, 1, axis=0)
            y = acc[8:, :] + cb_ref[:, cols]
            return y * _sigmoid(y) * scale
        run(conv, epi_cols=CONV_COLS)


def _inproj(layer, seq, x2d, norm_g, w_z, b_z, w_g, b_g, w_gt, b_gt, conv_w, conv_b, cos, sin):
    T = x2d.shape[0]
    tiles_per_seq = seq // TM_IN
    n_conv = Z_MV // TN_IN
    grid = (T // TM_IN, Z_COLS // TN_IN)
    return pl.pallas_call(
        functools.partial(_inproj_kernel, tiles_per_seq=tiles_per_seq),
        grid=grid,
        in_specs=[
            pl.BlockSpec((TM_IN, D_MODEL), lambda i, j: (i, 0)),
            pl.BlockSpec((None, 1, D_MODEL), lambda i, j: (layer, 0, 0)),
            pl.BlockSpec((None, D_MODEL, TN_IN), lambda i, j: (layer, 0, j)),
            pl.BlockSpec((None, 1, TN_IN), lambda i, j: (layer, 0, j)),
            pl.BlockSpec((None, D_MODEL, GATE_LANES), lambda i, j: (layer, 0, 0)),
            pl.BlockSpec((None, 1, GATE_LANES), lambda i, j: (layer, 0, 0)),
            pl.BlockSpec((None, GATE_ROWS, D_MODEL), lambda i, j: (layer, 0, 0)),
            pl.BlockSpec((None, GATE_ROWS, 1), lambda i, j: (layer, 0, 0)),
            pl.BlockSpec((None, CONV_K, TN_IN), lambda i, j: (layer, 0, jnp.minimum(j, n_conv - 1))),
            pl.BlockSpec((None, 1, TN_IN), lambda i, j: (layer, 0, jnp.minimum(j, n_conv - 1))),
            pl.BlockSpec((TM_IN, LANE), lambda i, j: (i % tiles_per_seq, 0)),
            pl.BlockSpec((TM_IN, LANE), lambda i, j: (i % tiles_per_seq, 0)),
        ],
        out_specs=[
            pl.BlockSpec((TM_IN, TN_IN), lambda i, j: (i, j)),
            pl.BlockSpec((TM_IN, GATE_LANES), lambda i, j: (i, 0)),
            pl.BlockSpec((GATE_ROWS, TM_IN), lambda i, j: (0, i)),
        ],
        out_shape=[
            jax.ShapeDtypeStruct((T, Z_COLS), BF16),
            jax.ShapeDtypeStruct((T, GATE_LANES), F32),
            jax.ShapeDtypeStruct((GATE_ROWS, T), F32),
        ],
        scratch_shapes=[
            pltpu.VMEM((TM_IN, D_MODEL), BF16),
            pltpu.VMEM((n_conv, 8, TN_IN), F32),
            pltpu.VMEM((2, SUB_IN, TN_IN), F32),
        ],
        compiler_params=pltpu.CompilerParams(
            dimension_semantics=("arbitrary", "arbitrary"),
            vmem_limit_bytes=VMEM_LIMIT),
        name="inproj",
    )(x2d, norm_g, w_z, b_z, w_g, b_g, w_gt, b_gt, conv_w, conv_b, cos, sin)


def _mixer_kernel(q_ref, k_ref, v_ref, o_ref, rq_ref, rk_ref, rv_ref, rg_ref,
                  gc_ref, gr_ref, intra_ref, rcol_ref, mg_ref, rgain_ref,
                  ym_ref, yr_ref,
                  hm_scr, hr_scr, c_scr, r_scr, m_scr, *, cdec):
    L = CHUNK_L
    t = pl.program_id(1)

    @pl.when(t == 0)
    def _():
        c_scr[...] = jnp.zeros(c_scr.shape, F32)
        r_scr[...] = jnp.zeros(r_scr.shape, F32)
        m_scr[...] = jnp.zeros(m_scr.shape, F32)

    row_id = lax.broadcasted_iota(jnp.int32, (L, L), 0)
    col_id = lax.broadcasted_iota(jnp.int32, (L, L), 1)
    causal = row_id >= col_id
    tril = causal.astype(BF16)
    triu = (row_id <= col_id).astype(BF16)
    gcol = gc_ref[...]
    grow = gr_ref[...]
    b_cols = _dot01_left(tril, _log_sigmoid(gcol))
    b_rows = _dot01_right(_log_sigmoid(grow), triu)
    ones_pad = jnp.ones((L, LANE), BF16)
    rcol = rcol_ref[...]

    for h in range(HEADS):
        sl = slice(h * M_DK, (h + 1) * M_DK)
        q = q_ref[:, sl]
        k = k_ref[:, sl]
        vaug = jnp.concatenate([v_ref[:, sl], ones_pad], axis=1)
        bcol = b_cols[:, HEADS + h:HEADS + h + 1]
        licol = gcol[:, h:h + 1]
        brow = b_rows[HEADS + h:HEADS + h + 1, :]
        lirow = grow[h:h + 1, :]
        m_prev = m_scr[h][0:1, 0:1]

        log_intra = jnp.where(causal, bcol - brow + lirow, -jnp.inf)
        inter = bcol + m_prev
        m_t = jnp.maximum(inter, jnp.max(log_intra, axis=-1, keepdims=True))
        dmat = jnp.exp(log_intra - m_t)
        d_inter = jnp.exp(inter - m_t)
        s = lax.dot_general(q, k, (((1,), (1,)), ((), ())), preferred_element_type=F32) * dmat
        c_old = c_scr[h]
        num = (jnp.dot(s.astype(BF16), vaug, preferred_element_type=F32)
               + d_inter * jnp.dot(q, c_old.astype(BF16), preferred_element_type=F32))
        den = num[:, M_DV:M_DV + 1]
        scale = 1.0 / jnp.maximum(jnp.abs(den), jnp.exp(-m_t))
        hm_scr[:, sl] = num[:, :M_DV] * scale

        b_end = bcol[L - 1:L, :]
        log_w = b_end - bcol + licol
        m_new = jnp.maximum(b_end + m_prev, jnp.max(log_w, axis=0, keepdims=True))
        w = jnp.exp(log_w - m_new)
        decay = jnp.exp(b_end + m_prev - m_new)
        kw = (k.astype(F32) * w).astype(BF16)
        c_scr[h] = decay * c_old + lax.dot_general(kw, vaug, (((0,), (0,)), ((), ())),
                                                   preferred_element_type=F32)
        m_scr[h] = jnp.broadcast_to(m_new, m_scr.shape[1:])

        sl = slice(h * R_DK, (h + 1) * R_DK)
        vs = slice(h * R_DV, (h + 1) * R_DV)
        rq = rq_ref[:, sl]
        rk = rk_ref[:, sl]
        rv = rv_ref[:, vs]
        s = lax.dot_general(rq, rk, (((1,), (1,)), ((), ())), preferred_element_type=F32) * intra_ref[h]
        r_old = r_scr[h]
        hr_scr[:, vs] = (jnp.dot(s.astype(BF16), rv, preferred_element_type=F32)
                         + jnp.dot(rq, r_old.astype(BF16), preferred_element_type=F32)
                         * rcol[:, h:h + 1])
        rkw = (rk.astype(F32) * rcol[:, HEADS + h:HEADS + h + 1]).astype(BF16)
        r_scr[h] = cdec[h] * r_old + lax.dot_general(rkw, rv, (((0,), (0,)), ((), ())),
                                                     preferred_element_type=F32)

    mg = mg_ref[...]
    rgain = rgain_ref[...]

    def head_norm(x):
        mu = jnp.mean(x, axis=-1, keepdims=True)
        xc = x - mu
        var = jnp.mean(xc * xc, axis=-1, keepdims=True)
        return xc * lax.rsqrt(var + EPS)

    def finish(rows):
        for h in range(HEADS):
            sl = slice(h * M_DV, (h + 1) * M_DV)
            y = o_ref[rows, sl].astype(F32) * (head_norm(hm_scr[rows, sl]) * mg[:, sl])
            ym_ref[rows, sl] = y.astype(BF16)
            vs = slice(h * R_DV, (h + 1) * R_DV)
            y = rg_ref[rows, vs].astype(F32) * (head_norm(hr_scr[rows, vs]) * rgain[:, vs])
            yr_ref[rows, vs] = y.astype(BF16)
    _strips(L, finish, unroll=2)


def _retention_tables(L):
    gamma = 1.0 - np.exp2(-5.0 - np.arange(HEADS, dtype=np.float64))
    lg = np.log(gamma)
    pos = np.arange(L, dtype=np.float64)
    diff = pos[:, None] - pos[None, :]
    intra = np.where(diff >= 0, np.exp(np.maximum(diff, 0.0) * lg[:, None, None]), 0.0)
    rcol = np.zeros((L, LANE), np.float64)
    rcol[:, :HEADS] = np.exp((pos[:, None] + 1.0) * lg[None, :])
    rcol[:, HEADS:2 * HEADS] = np.exp((L - 1.0 - pos[:, None]) * lg[None, :])
    cdec = tuple(float(v) for v in np.exp(L * lg))
    return jnp.asarray(intra, F32), jnp.asarray(rcol, F32), cdec


def _mixer(layer, batch, seq, z, gcol, grow, m_gain, r_gain):
    L = CHUNK_L
    nt = seq // L
    intra, rcol, cdec = _retention_tables(L)

    def zspec(width, col0):
        blk = col0 // width
        return pl.BlockSpec((L, width), lambda b, t: (b * nt + t, blk))

    def const(shape):
        return pl.BlockSpec(shape, lambda b, t: tuple(0 for _ in shape))

    def per_layer(width):
        return pl.BlockSpec((None, 1, width), lambda b, t: (layer, 0, 0))

    T = batch * seq
    return pl.pallas_call(
        functools.partial(_mixer_kernel, cdec=cdec),
        grid=(batch, nt),
        in_specs=[
            zspec(M_QK, Z_MQ), zspec(M_QK, Z_MK), zspec(M_V, Z_MV), zspec(M_V, Z_MO),
            zspec(R_QK, Z_RQ), zspec(R_QK, Z_RK), zspec(R_V, Z_RV), zspec(R_V, Z_RG),
            pl.BlockSpec((L, GATE_LANES), lambda b, t: (b * nt + t, 0)),
            pl.BlockSpec((GATE_ROWS, L), lambda b, t: (0, b * nt + t)),
            const((HEADS, L, L)), const((L, LANE)),
            per_layer(M_V), per_layer(R_V),
        ],
        out_specs=[
            pl.BlockSpec((L, M_V), lambda b, t: (b * nt + t, 0)),
            pl.BlockSpec((L, R_V), lambda b, t: (b * nt + t, 0)),
        ],
        out_shape=[
            jax.ShapeDtypeStruct((T, M_V), BF16),
            jax.ShapeDtypeStruct((T, R_V), BF16),
        ],
        scratch_shapes=[
            pltpu.VMEM((L, M_V), F32), pltpu.VMEM((L, R_V), F32),
            pltpu.VMEM((HEADS, M_DK, M_DV + LANE), F32),
            pltpu.VMEM((HEADS, R_DK, R_DV), F32),
            pltpu.VMEM((HEADS, 8, LANE), F32),
        ],
        compiler_params=pltpu.CompilerParams(
            dimension_semantics=("arbitrary", "arbitrary"),
            vmem_limit_bytes=VMEM_LIMIT),
        name="mixer",
    )(z, z, z, z, z, z, z, z, gcol, grow, intra, rcol, m_gain, r_gain)


def _rmsnorm_rows(src_ref, dst_ref, g, n_rows):
    def norm(rows):
        x = src_ref[rows, :]
        ms = jnp.mean(x * x, axis=-1, keepdims=True)
        dst_ref[rows, :] = (x * lax.rsqrt(ms + EPS) * g).astype(dst_ref.dtype)
    _strips(n_rows, norm, unroll=8)


def _post_kernel(ym_ref, yr_ref, gm_ref, gr_ref, x_ref, p_ref,
                 wbm_ref, wbr_ref, wout_ref, g2_ref, w1_ref, b1_ref, w2_ref, b2_ref,
                 g3_ref, wpg_ref, wpe_ref, gf_ref,
                 out_ref, x_scr, h_scr, *, final):
    a = jnp.dot(ym_ref[...], wbm_ref[...], preferred_element_type=F32)
    merged = gm_ref[...].astype(F32) * a
    b = jnp.dot(yr_ref[...], wbr_ref[...], preferred_element_type=F32)
    merged = merged + gr_ref[...].astype(F32) * b
    x_scr[...] = x_ref[...] + jnp.dot(merged.astype(BF16), wout_ref[...], preferred_element_type=F32)

    _rmsnorm_rows(x_scr, h_scr, g2_ref[...], TM_POST)
    h = h_scr[...]
    acc = x_scr[...] + b2_ref[...]
    for c in range(D_FF // D_MODEL):
        cs = slice(c * D_MODEL, (c + 1) * D_MODEL)
        u = jnp.maximum(jnp.dot(h, w1_ref[:, cs], preferred_element_type=F32) + b1_ref[:, cs], 0.0)
        acc = acc + jnp.dot((u * u).astype(BF16), w2_ref[cs, :], preferred_element_type=F32)
    x_scr[...] = acc

    _rmsnorm_rows(x_scr, h_scr, g3_ref[...], TM_POST)
    gate = _sigmoid(jnp.dot(h_scr[...], wpg_ref[...], preferred_element_type=F32))
    pe = jnp.dot(p_ref[...].astype(BF16), wpe_ref[...], preferred_element_type=F32)
    if final:
        x_scr[...] = x_scr[...] + gate * pe
        _rmsnorm_rows(x_scr, out_ref, gf_ref[...], TM_POST)
    else:
        out_ref[...] = x_scr[...] + gate * pe


def _post(layer, final, ym, yr, z, x2d, p3d, w_bm, w_br, w_out, g2, w1, b1, w2, b2, g3, wpg, wpe, gf):
    T = x2d.shape[0]
    tm = TM_POST

    def rows(width, blk=0):
        return pl.BlockSpec((tm, width), lambda i: (i, blk))

    def weight(k, n):
        return pl.BlockSpec((None, k, n), lambda i: (layer, 0, 0), pipeline_mode=pl.Buffered(1))

    def vec(n):
        return pl.BlockSpec((None, 1, n), lambda i: (layer, 0, 0))

    return pl.pallas_call(
        functools.partial(_post_kernel, final=final),
        grid=(T // tm,),
        in_specs=[
            rows(M_V), rows(R_V), rows(D_MODEL, Z_GM // D_MODEL), rows(D_MODEL, Z_GR // D_MODEL),
            rows(D_MODEL),
            pl.BlockSpec((None, tm, PE_DIM), lambda i: (layer, i, 0)),
            weight(M_V, D_MODEL), weight(R_V, D_MODEL), weight(D_MODEL, D_MODEL),
            vec(D_MODEL), weight(D_MODEL, D_FF), vec(D_FF), weight(D_FF, D_MODEL), vec(D_MODEL),
            vec(D_MODEL), weight(D_MODEL, D_MODEL), weight(PE_DIM, D_MODEL),
            pl.BlockSpec((1, D_MODEL), lambda i: (0, 0)),
        ],
        out_specs=rows(D_MODEL),
        out_shape=jax.ShapeDtypeStruct((T, D_MODEL), F32),
        scratch_shapes=[pltpu.VMEM((tm, D_MODEL), F32), pltpu.VMEM((tm, D_MODEL), BF16)],
        compiler_params=pltpu.CompilerParams(
            dimension_semantics=("arbitrary",),
            vmem_limit_bytes=VMEM_LIMIT),
        name="post_final" if final else "post",
    )(ym, yr, z, z, x2d, p3d, w_bm, w_br, w_out, g2, w1, b1, w2, b2, g3, wpg, wpe, gf)


def _reorder_in_proj(w_in, b_in):
    big = [(0, 4096), (4104, 4104 + 6144 + 2048)]
    w_z = jnp.concatenate([w_in[..., a:b] for a, b in big], axis=-1)
    b_z = jnp.concatenate([b_in[..., a:b] for a, b in big], axis=-1)
    w_gate = w_in[..., 4096:4104]
    b_gate = b_in[..., 4096:4104]
    return w_z, b_z, w_gate, b_gate


def kernel(x, p, norm1_g, w_in, b_in, conv_w, conv_b, m_norm_g, r_norm_g, w_bm, w_br, w_out,
           norm2_g, w_ff1, b_ff1, w_ff2, b_ff2, norm3_g, w_pe_gate, w_pe, final_g):
    B, S, _ = x.shape
    depth = w_in.shape[0]
    T = B * S
    assert S % CHUNK_L == 0 and S % TM_IN == 0 and T % TM_POST == 0

    w_z, b_z, w_gate, b_gate = _reorder_in_proj(w_in, b_in)
    w_z = w_z.astype(BF16)
    b_z = b_z[:, None, :]
    pad = GATE_LANES - GATE_ROWS
    w_g = jnp.pad(w_gate, ((0, 0), (0, 0), (0, pad))).astype(BF16)
    b_g = jnp.pad(b_gate, ((0, 0), (0, pad)))[:, None, :]
    w_gt = jnp.swapaxes(w_gate, 1, 2).astype(BF16)
    b_gt = b_gate[:, :, None]

    pos = jnp.arange(S, dtype=F32)
    inv_freq = ROPE_BASE ** (-jnp.arange(0, R_DK, 2, dtype=F32) / R_DK)
    ang = pos[:, None] * inv_freq[None, :]
    cos = jnp.cos(ang)
    sin = jnp.sin(ang)

    bf = lambda w: w.astype(BF16)
    row = lambda v: v[:, None, :]
    w_bm, w_br, w_out, w_ff1, w_ff2, w_pe_gate, w_pe = map(
        bf, (w_bm, w_br, w_out, w_ff1, w_ff2, w_pe_gate, w_pe))
    norm1_g, conv_b, m_norm_g, r_norm_g, norm2_g, b_ff1, b_ff2, norm3_g = map(
        row, (norm1_g, conv_b, m_norm_g, r_norm_g, norm2_g, b_ff1, b_ff2, norm3_g))
    final_g = final_g[None, :]

    x2d = x.reshape(T, D_MODEL)
    p3d = p.reshape(depth, T, PE_DIM)
    for i in range(depth):
        z, gcol, grow = _inproj(i, S, x2d, norm1_g, w_z, b_z, w_g, b_g, w_gt, b_gt,
                                conv_w, conv_b, cos, sin)
        ym, yr = _mixer(i, B, S, z, gcol, grow, m_norm_g, r_norm_g)
        x2d = _post(i, i == depth - 1, ym, yr, z, x2d, p3d, w_bm, w_br, w_out,
                    norm2_g, w_ff1, b_ff1, w_ff2, b_ff2, norm3_g, w_pe_gate, w_pe, final_g)
    return x2d.reshape(B, S, D_MODEL)
```

```python
import functools

import numpy as np
import jax
import jax.numpy as jnp
from jax import lax
from jax.experimental import pallas as pl
from jax.experimental.pallas import tpu as pltpu

F32 = jnp.float32
BF16 = jnp.bfloat16

D_MODEL = 1024
PE_DIM = 256
EPS = 1e-6
HEADS = 4
M_DK = 256
M_DV = 256
R_DK = 256
R_DV = 512
M_QK = HEADS * M_DK
M_V = HEADS * M_DV
R_QK = HEADS * R_DK
R_V = HEADS * R_DV
CONV_K = 4
ROPE_BASE = 10000.0
D_FF = 4 * D_MODEL

Z_MQ, Z_MK, Z_MV, Z_MO = 0, 1024, 2048, 3072
Z_RQ, Z_RK, Z_RV, Z_RG = 4096, 5120, 6144, 8192
Z_GM, Z_GR = 10240, 11264
Z_COLS = 12288
GATE_LANES = 128
GATE_ROWS = 8

LANE = 128
VMEM_LIMIT = 60 * 1024 * 1024

TM_IN = 2048
TN_IN = 2048
CHUNK_L = 256
TS_MIX = 512
TM_POST = 256
STRIP = 16
CONV_ROWS = 32
CONV_COLS = 512


def _sigmoid(x):
    return 1.0 / (1.0 + jnp.exp(-x))


def _log_sigmoid(x):
    return jnp.minimum(x, 0.0) - jnp.log1p(jnp.exp(-jnp.abs(x)))


def _split3(x):
    hi = x.astype(BF16)
    r1 = x - hi.astype(F32)
    mid = r1.astype(BF16)
    lo = (r1 - mid.astype(F32)).astype(BF16)
    return hi, mid, lo


def _dot01_left(mat01, x):
    return sum(jnp.dot(mat01, piece, preferred_element_type=F32) for piece in _split3(x))


def _dot01_right(x, mat01):
    return sum(jnp.dot(piece, mat01, preferred_element_type=F32) for piece in _split3(x))


def _strips(n_rows, body, rows_per_strip=STRIP, unroll=1):
    def step(i, carry):
        body(pl.ds(pl.multiple_of(i * rows_per_strip, rows_per_strip), rows_per_strip))
        return carry
    lax.fori_loop(0, n_rows // rows_per_strip, step, 0, unroll=unroll)


def _inproj_kernel(x_ref, g_ref, w_ref, b_ref, wg_ref, bg_ref, wgt_ref, bgt_ref,
                   z_ref, gc_ref, gr_ref, h_scr):
    @pl.when(pl.program_id(1) == 0)
    def _():
        g = g_ref[...]

        def norm(rows):
            x = x_ref[rows, :]
            ms = jnp.mean(x * x, axis=-1, keepdims=True)
            h_scr[rows, :] = (x * lax.rsqrt(ms + EPS) * g).astype(BF16)
        _strips(TM_IN, norm, unroll=8)
        h = h_scr[...]
        gc_ref[...] = jnp.dot(h, wg_ref[...], preferred_element_type=F32) + bg_ref[...]
        gr_ref[...] = lax.dot_general(wgt_ref[...], h, (((1,), (1,)), ((), ())),
                                      preferred_element_type=F32) + bgt_ref[...]

    z = jnp.dot(h_scr[...], w_ref[...], preferred_element_type=F32) + b_ref[...]
    z_ref[...] = z.astype(z_ref.dtype)


def _inproj(layer, x2d, norm_g, w_z, b_z, w_g, b_g, w_gt, b_gt):
    T = x2d.shape[0]
    grid = (T // TM_IN, Z_COLS // TN_IN)
    return pl.pallas_call(
        _inproj_kernel,
        grid=grid,
        in_specs=[
            pl.BlockSpec((TM_IN, D_MODEL), lambda i, j: (i, 0)),
            pl.BlockSpec((None, 1, D_MODEL), lambda i, j: (layer, 0, 0)),
            pl.BlockSpec((None, D_MODEL, TN_IN), lambda i, j: (layer, 0, j)),
            pl.BlockSpec((None, 1, TN_IN), lambda i, j: (layer, 0, j)),
            pl.BlockSpec((None, D_MODEL, GATE_LANES), lambda i, j: (layer, 0, 0)),
            pl.BlockSpec((None, 1, GATE_LANES), lambda i, j: (layer, 0, 0)),
            pl.BlockSpec((None, GATE_ROWS, D_MODEL), lambda i, j: (layer, 0, 0)),
            pl.BlockSpec((None, GATE_ROWS, 1), lambda i, j: (layer, 0, 0)),
        ],
        out_specs=[
            pl.BlockSpec((TM_IN, TN_IN), lambda i, j: (i, j)),
            pl.BlockSpec((TM_IN, GATE_LANES), lambda i, j: (i, 0)),
            pl.BlockSpec((GATE_ROWS, TM_IN), lambda i, j: (0, i)),
        ],
        out_shape=[
            jax.ShapeDtypeStruct((T, Z_COLS), BF16),
            jax.ShapeDtypeStruct((T, GATE_LANES), F32),
            jax.ShapeDtypeStruct((GATE_ROWS, T), F32),
        ],
        scratch_shapes=[pltpu.VMEM((TM_IN, D_MODEL), BF16)],
        compiler_params=pltpu.CompilerParams(
            dimension_semantics=("arbitrary", "arbitrary"),
            vmem_limit_bytes=VMEM_LIMIT),
        name="inproj",
    )(x2d, norm_g, w_z, b_z, w_g, b_g, w_gt, b_gt)


def _mixer_kernel(zqk_ref, zv_ref, zo_ref, rq_ref, rk_ref, rv_ref, rg_ref,
                  gc_ref, gr_ref, cw_ref, cb_ref, cos_ref, sin_ref,
                  intra_ref, rcol_ref, mg_ref, rgain_ref,
                  ym_ref, yr_ref,
                  conv_scr, q_scr, k_scr, rqs_scr, rks_scr, hm_scr, hr_scr,
                  c_scr, r_scr, m_scr, *, cdec):
    L = CHUNK_L
    TS = TS_MIX
    t = pl.program_id(1)

    @pl.when(t == 0)
    def _():
        conv_scr[0:8, :] = jnp.zeros((8, 2 * M_QK), F32)
        c_scr[...] = jnp.zeros(c_scr.shape, F32)
        r_scr[...] = jnp.zeros(r_scr.shape, F32)
        m_scr[...] = jnp.zeros(m_scr.shape, F32)

    conv_scr[8:8 + TS, :] = zqk_ref[...].astype(F32)
    cw = cw_ref[...]
    cb = cb_ref[...]

    def conv(rows):
        for c0 in range(0, 2 * M_QK, CONV_COLS):
            cols = slice(c0, c0 + CONV_COLS)
            win = conv_scr[pl.ds(rows.start, CONV_ROWS + 8), cols]
            acc = win * cw[0:1, cols]
            for j in range(1, CONV_K):
                acc = win * cw[j:j + 1, cols] + pltpu.roll(acc, 1, axis=0)
            y = acc[8:, :] + cb[:, cols]
            y = y * _sigmoid(y)
            if c0 < M_QK:
                q_scr[rows, cols] = (y * (M_DK ** -0.5)).astype(BF16)
            else:
                k_scr[rows, slice(c0 - M_QK, c0 - M_QK + CONV_COLS)] = y.astype(BF16)
    _strips(TS, conv, rows_per_strip=CONV_ROWS)
    conv_scr[0:8, :] = conv_scr[TS:TS + 8, :]

    def rope(rows):
        c = cos_ref[rows, :]
        s = sin_ref[rows, :]
        for src, dst, scale in ((rq_ref, rqs_scr, 1.0), (rk_ref, rks_scr, R_DK ** -0.5)):
            for h in range(HEADS):
                lo = h * R_DK
                x1 = src[rows, lo:lo + LANE].astype(F32)
                x2 = src[rows, lo + LANE:lo + 2 * LANE].astype(F32)
                dst[rows, lo:lo + LANE] = ((x1 * c - x2 * s) * scale).astype(BF16)
                dst[rows, lo + LANE:lo + 2 * LANE] = ((x1 * s + x2 * c) * scale).astype(BF16)
    _strips(TS, rope, unroll=2)

    row_id = lax.broadcasted_iota(jnp.int32, (L, L), 0)
    col_id = lax.broadcasted_iota(jnp.int32, (L, L), 1)
    causal = row_id >= col_id
    tril = causal.astype(BF16)
    triu = (row_id <= col_id).astype(BF16)
    ones_pad = jnp.ones((L, LANE), BF16)
    rcol = rcol_ref[...]

    for c in range(TS // L):
        rows = slice(c * L, (c + 1) * L)
        gcol = gc_ref[rows, :]
        grow = gr_ref[:, rows]
        b_cols = _dot01_left(tril, _log_sigmoid(gcol))
        b_rows = _dot01_right(_log_sigmoid(grow), triu)
        for h in range(HEADS):
            sl = slice(h * M_DK, (h + 1) * M_DK)
            q = q_scr[rows, sl]
            k = k_scr[rows, sl]
            vaug = jnp.concatenate([zv_ref[rows, sl], ones_pad], axis=1)
            bcol = b_cols[:, HEADS + h:HEADS + h + 1]
            licol = gcol[:, h:h + 1]
            brow = b_rows[HEADS + h:HEADS + h + 1, :]
            lirow = grow[h:h + 1, :]
            m_prev = m_scr[h][0:1, 0:1]

            log_intra = jnp.where(causal, bcol - brow + lirow, -jnp.inf)
            inter = bcol + m_prev
            m_t = jnp.maximum(inter, jnp.max(log_intra, axis=-1, keepdims=True))
            dmat = jnp.exp(log_intra - m_t)
            d_inter = jnp.exp(inter - m_t)
            s = lax.dot_general(q, k, (((1,), (1,)), ((), ())), preferred_element_type=F32) * dmat
            c_old = c_scr[h]
            num = (jnp.dot(s.astype(BF16), vaug, preferred_element_type=F32)
                   + d_inter * jnp.dot(q, c_old.astype(BF16), preferred_element_type=F32))
            den = num[:, M_DV:M_DV + 1]
            scale = 1.0 / jnp.maximum(jnp.abs(den), jnp.exp(-m_t))
            hm_scr[rows, sl] = num[:, :M_DV] * scale

            b_end = bcol[L - 1:L, :]
            log_w = b_end - bcol + licol
            m_new = jnp.maximum(b_end + m_prev, jnp.max(log_w, axis=0, keepdims=True))
            w = jnp.exp(log_w - m_new)
            decay = jnp.exp(b_end + m_prev - m_new)
            kw = (k.astype(F32) * w).astype(BF16)
            c_scr[h] = decay * c_old + lax.dot_general(kw, vaug, (((0,), (0,)), ((), ())),
                                                       preferred_element_type=F32)
            m_scr[h] = jnp.broadcast_to(m_new, m_scr.shape[1:])

            sl = slice(h * R_DK, (h + 1) * R_DK)
            vs = slice(h * R_DV, (h + 1) * R_DV)
            rq = rqs_scr[rows, sl]
            rk = rks_scr[rows, sl]
            rv = rv_ref[rows, vs]
            s = lax.dot_general(rq, rk, (((1,), (1,)), ((), ())), preferred_element_type=F32) * intra_ref[h]
            r_old = r_scr[h]
            hr_scr[rows, vs] = (jnp.dot(s.astype(BF16), rv, preferred_element_type=F32)
                                + jnp.dot(rq, r_old.astype(BF16), preferred_element_type=F32)
                                * rcol[:, h:h + 1])
            rkw = (rk.astype(F32) * rcol[:, HEADS + h:HEADS + h + 1]).astype(BF16)
            r_scr[h] = cdec[h] * r_old + lax.dot_general(rkw, rv, (((0,), (0,)), ((), ())),
                                                         preferred_element_type=F32)

    mg = mg_ref[...]
    rgain = rgain_ref[...]

    def head_norm(x):
        mu = jnp.mean(x, axis=-1, keepdims=True)
        xc = x - mu
        var = jnp.mean(xc * xc, axis=-1, keepdims=True)
        return xc * lax.rsqrt(var + EPS)

    def finish(rows):
        for h in range(HEADS):
            sl = slice(h * M_DV, (h + 1) * M_DV)
            y = _sigmoid(zo_ref[rows, sl].astype(F32)) * (head_norm(hm_scr[rows, sl]) * mg[:, sl])
            ym_ref[rows, sl] = y.astype(BF16)
            vs = slice(h * R_DV, (h + 1) * R_DV)
            g = rg_ref[rows, vs].astype(F32)
            y = (g * _sigmoid(g)) * (head_norm(hr_scr[rows, vs]) * rgain[:, vs])
            yr_ref[rows, vs] = y.astype(BF16)
    _strips(TS, finish, unroll=2)


def _retention_tables(L):
    gamma = 1.0 - np.exp2(-5.0 - np.arange(HEADS, dtype=np.float64))
    lg = np.log(gamma)
    pos = np.arange(L, dtype=np.float64)
    diff = pos[:, None] - pos[None, :]
    intra = np.where(diff >= 0, np.exp(np.maximum(diff, 0.0) * lg[:, None, None]), 0.0)
    rcol = np.zeros((L, LANE), np.float64)
    rcol[:, :HEADS] = np.exp((pos[:, None] + 1.0) * lg[None, :])
    rcol[:, HEADS:2 * HEADS] = np.exp((L - 1.0 - pos[:, None]) * lg[None, :])
    cdec = tuple(float(v) for v in np.exp(L * lg))
    return jnp.asarray(intra, F32), jnp.asarray(rcol, F32), cdec


def _mixer(layer, batch, seq, z, gcol, grow, conv_w, conv_b, cos, sin, m_gain, r_gain):
    L = CHUNK_L
    TS = TS_MIX
    nt = seq // TS
    intra, rcol, cdec = _retention_tables(L)

    def zspec(width, col0):
        blk = col0 // width
        return pl.BlockSpec((TS, width), lambda b, t: (b * nt + t, blk))

    def const(shape):
        return pl.BlockSpec(shape, lambda b, t: tuple(0 for _ in shape))

    def per_layer(width):
        return pl.BlockSpec((None, 1, width), lambda b, t: (layer, 0, 0))

    T = batch * seq
    return pl.pallas_call(
        functools.partial(_mixer_kernel, cdec=cdec),
        grid=(batch, nt),
        in_specs=[
            zspec(2 * M_QK, Z_MQ), zspec(M_V, Z_MV), zspec(M_V, Z_MO),
            zspec(R_QK, Z_RQ), zspec(R_QK, Z_RK), zspec(R_V, Z_RV), zspec(R_V, Z_RG),
            pl.BlockSpec((TS, GATE_LANES), lambda b, t: (b * nt + t, 0)),
            pl.BlockSpec((GATE_ROWS, TS), lambda b, t: (0, b * nt + t)),
            pl.BlockSpec((None, CONV_K, 2 * M_QK), lambda b, t: (layer, 0, 0)),
            per_layer(2 * M_QK),
            pl.BlockSpec((TS, LANE), lambda b, t: (t, 0)),
            pl.BlockSpec((TS, LANE), lambda b, t: (t, 0)),
            const((HEADS, L, L)), const((L, LANE)),
            per_layer(M_V), per_layer(R_V),
        ],
        out_specs=[
            pl.BlockSpec((TS, M_V), lambda b, t: (b * nt + t, 0)),
            pl.BlockSpec((TS, R_V), lambda b, t: (b * nt + t, 0)),
        ],
        out_shape=[
            jax.ShapeDtypeStruct((T, M_V), BF16),
            jax.ShapeDtypeStruct((T, R_V), BF16),
        ],
        scratch_shapes=[
            pltpu.VMEM((TS + 8, 2 * M_QK), F32),
            pltpu.VMEM((TS, M_QK), BF16), pltpu.VMEM((TS, M_QK), BF16),
            pltpu.VMEM((TS, R_QK), BF16), pltpu.VMEM((TS, R_QK), BF16),
            pltpu.VMEM((TS, M_V), F32), pltpu.VMEM((TS, R_V), F32),
            pltpu.VMEM((HEADS, M_DK, M_DV + LANE), F32),
            pltpu.VMEM((HEADS, R_DK, R_DV), F32),
            pltpu.VMEM((HEADS, 8, LANE), F32),
        ],
        compiler_params=pltpu.CompilerParams(
            dimension_semantics=("arbitrary", "arbitrary"),
            vmem_limit_bytes=VMEM_LIMIT),
        name="mixer",
    )(z, z, z, z, z, z, z, gcol, grow, conv_w, conv_b, cos, sin, intra, rcol, m_gain, r_gain)


def _rmsnorm_rows(src_ref, dst_ref, g, n_rows):
    def norm(rows):
        x = src_ref[rows, :]
        ms = jnp.mean(x * x, axis=-1, keepdims=True)
        dst_ref[rows, :] = (x * lax.rsqrt(ms + EPS) * g).astype(dst_ref.dtype)
    _strips(n_rows, norm, unroll=8)


def _post_kernel(ym_ref, yr_ref, gm_ref, gr_ref, x_ref, p_ref,
                 wbm_ref, wbr_ref, wout_ref, g2_ref, w1_ref, b1_ref, w2_ref, b2_ref,
                 g3_ref, wpg_ref, wpe_ref, gf_ref,
                 out_ref, x_scr, h_scr, *, final):
    a = jnp.dot(ym_ref[...], wbm_ref[...], preferred_element_type=F32)
    merged = _sigmoid(gm_ref[...].astype(F32)) * a
    b = jnp.dot(yr_ref[...], wbr_ref[...], preferred_element_type=F32)
    merged = merged + _sigmoid(gr_ref[...].astype(F32)) * b
    x_scr[...] = x_ref[...] + jnp.dot(merged.astype(BF16), wout_ref[...], preferred_element_type=F32)

    _rmsnorm_rows(x_scr, h_scr, g2_ref[...], TM_POST)
    h = h_scr[...]
    acc = x_scr[...] + b2_ref[...]
    for c in range(D_FF // D_MODEL):
        cs = slice(c * D_MODEL, (c + 1) * D_MODEL)
        u = jnp.maximum(jnp.dot(h, w1_ref[:, cs], preferred_element_type=F32) + b1_ref[:, cs], 0.0)
        acc = acc + jnp.dot((u * u).astype(BF16), w2_ref[cs, :], preferred_element_type=F32)
    x_scr[...] = acc

    _rmsnorm_rows(x_scr, h_scr, g3_ref[...], TM_POST)
    gate = _sigmoid(jnp.dot(h_scr[...], wpg_ref[...], preferred_element_type=F32))
    pe = jnp.dot(p_ref[...].astype(BF16), wpe_ref[...], preferred_element_type=F32)
    if final:
        x_scr[...] = x_scr[...] + gate * pe
        _rmsnorm_rows(x_scr, out_ref, gf_ref[...], TM_POST)
    else:
        out_ref[...] = x_scr[...] + gate * pe


def _post(layer, final, ym, yr, z, x2d, p3d, w_bm, w_br, w_out, g2, w1, b1, w2, b2, g3, wpg, wpe, gf):
    T = x2d.shape[0]
    tm = TM_POST

    def rows(width, blk=0):
        return pl.BlockSpec((tm, width), lambda i: (i, blk))

    def weight(k, n):
        return pl.BlockSpec((None, k, n), lambda i: (layer, 0, 0), pipeline_mode=pl.Buffered(1))

    def vec(n):
        return pl.BlockSpec((None, 1, n), lambda i: (layer, 0, 0))

    return pl.pallas_call(
        functools.partial(_post_kernel, final=final),
        grid=(T // tm,),
        in_specs=[
            rows(M_V), rows(R_V), rows(D_MODEL, Z_GM // D_MODEL), rows(D_MODEL, Z_GR // D_MODEL),
            rows(D_MODEL),
            pl.BlockSpec((None, tm, PE_DIM), lambda i: (layer, i, 0)),
            weight(M_V, D_MODEL), weight(R_V, D_MODEL), weight(D_MODEL, D_MODEL),
            vec(D_MODEL), weight(D_MODEL, D_FF), vec(D_FF), weight(D_FF, D_MODEL), vec(D_MODEL),
            vec(D_MODEL), weight(D_MODEL, D_MODEL), weight(PE_DIM, D_MODEL),
            pl.BlockSpec((1, D_MODEL), lambda i: (0, 0)),
        ],
        out_specs=rows(D_MODEL),
        out_shape=jax.ShapeDtypeStruct((T, D_MODEL), F32),
        scratch_shapes=[pltpu.VMEM((tm, D_MODEL), F32), pltpu.VMEM((tm, D_MODEL), BF16)],
        compiler_params=pltpu.CompilerParams(
            dimension_semantics=("arbitrary",),
            vmem_limit_bytes=VMEM_LIMIT),
        name="post_final" if final else "post",
    )(ym, yr, z, z, x2d, p3d, w_bm, w_br, w_out, g2, w1, b1, w2, b2, g3, wpg, wpe, gf)


def _reorder_in_proj(w_in, b_in):
    big = [(0, 4096), (4104, 4104 + 6144 + 2048)]
    w_z = jnp.concatenate([w_in[..., a:b] for a, b in big], axis=-1)
    b_z = jnp.concatenate([b_in[..., a:b] for a, b in big], axis=-1)
    w_gate = w_in[..., 4096:4104]
    b_gate = b_in[..., 4096:4104]
    return w_z, b_z, w_gate, b_gate


def kernel(x, p, norm1_g, w_in, b_in, conv_w, conv_b, m_norm_g, r_norm_g, w_bm, w_br, w_out,
           norm2_g, w_ff1, b_ff1, w_ff2, b_ff2, norm3_g, w_pe_gate, w_pe, final_g):
    B, S, _ = x.shape
    depth = w_in.shape[0]
    T = B * S
    assert S % TS_MIX == 0 and TS_MIX % CHUNK_L == 0 and T % TM_IN == 0 and T % TM_POST == 0

    w_z, b_z, w_gate, b_gate = _reorder_in_proj(w_in, b_in)
    w_z = w_z.astype(BF16)
    b_z = b_z[:, None, :]
    pad = GATE_LANES - GATE_ROWS
    w_g = jnp.pad(w_gate, ((0, 0), (0, 0), (0, pad))).astype(BF16)
    b_g = jnp.pad(b_gate, ((0, 0), (0, pad)))[:, None, :]
    w_gt = jnp.swapaxes(w_gate, 1, 2).astype(BF16)
    b_gt = b_gate[:, :, None]

    pos = jnp.arange(S, dtype=F32)
    inv_freq = ROPE_BASE ** (-jnp.arange(0, R_DK, 2, dtype=F32) / R_DK)
    ang = pos[:, None] * inv_freq[None, :]
    cos = jnp.cos(ang)
    sin = jnp.sin(ang)

    bf = lambda w: w.astype(BF16)
    row = lambda v: v[:, None, :]
    w_bm, w_br, w_out, w_ff1, w_ff2, w_pe_gate, w_pe = map(
        bf, (w_bm, w_br, w_out, w_ff1, w_ff2, w_pe_gate, w_pe))
    norm1_g, conv_b, m_norm_g, r_norm_g, norm2_g, b_ff1, b_ff2, norm3_g = map(
        row, (norm1_g, conv_b, m_norm_g, r_norm_g, norm2_g, b_ff1, b_ff2, norm3_g))
    final_g = final_g[None, :]

    x2d = x.reshape(T, D_MODEL)
    p3d = p.reshape(depth, T, PE_DIM)
    for i in range(depth):
        z, gcol, grow = _inproj(i, x2d, norm1_g, w_z, b_z, w_g, b_g, w_gt, b_gt)
        ym, yr = _mixer(i, B, S, z, gcol, grow, conv_w, conv_b, cos, sin, m_norm_g, r_norm_g)
        x2d = _post(i, i == depth - 1, ym, yr, z, x2d, p3d, w_bm, w_br, w_out,
                    norm2_g, w_ff1, b_ff1, w_ff2, b_ff2, norm3_g, w_pe_gate, w_pe, final_g)
    return x2d.reshape(B, S, D_MODEL)
```

```python
import functools

import numpy as np
import jax
import jax.numpy as jnp
from jax import lax
from jax.experimental import pallas as pl
from jax.experimental.pallas import tpu as pltpu

F32 = jnp.float32
BF16 = jnp.bfloat16

D_MODEL = 1024
PE_DIM = 256
EPS = 1e-6
HEADS = 4
M_DK = 256
M_DV = 256
R_DK = 256
R_DV = 512
M_QK = HEADS * M_DK
M_V = HEADS * M_DV
R_QK = HEADS * R_DK
R_V = HEADS * R_DV
CONV_K = 4
ROPE_BASE = 10000.0
D_FF = 4 * D_MODEL

Z_MQ, Z_MK, Z_MV, Z_MO = 0, 1024, 2048, 3072
Z_RQ, Z_RK, Z_RV, Z_RG = 4096, 5120, 6144, 8192
Z_GM, Z_GR = 10240, 11264
Z_COLS = 12288
GATE_LANES = 128
GATE_ROWS = 8

LANE = 128
VMEM_LIMIT = 60 * 1024 * 1024

TM_IN = 2048
TN_IN = 1024
SUB_IN = 512
EPI_COLS = 256
CONV_COLS = 128
CHUNK_L = 256
TS_MIX = 512
TM_POST = 256
STRIP = 16


def _sigmoid(x):
    return 1.0 / (1.0 + jnp.exp(-x))


def _log_sigmoid(x):
    return jnp.minimum(x, 0.0) - jnp.log1p(jnp.exp(-jnp.abs(x)))


def _split3(x):
    hi = x.astype(BF16)
    r1 = x - hi.astype(F32)
    mid = r1.astype(BF16)
    lo = (r1 - mid.astype(F32)).astype(BF16)
    return hi, mid, lo


def _dot01_left(mat01, x):
    return sum(jnp.dot(mat01, piece, preferred_element_type=F32) for piece in _split3(x))


def _dot01_right(x, mat01):
    return sum(jnp.dot(piece, mat01, preferred_element_type=F32) for piece in _split3(x))


def _strips(n_rows, body, rows_per_strip=STRIP, unroll=1):
    def step(i, carry):
        body(pl.ds(pl.multiple_of(i * rows_per_strip, rows_per_strip), rows_per_strip))
        return carry
    lax.fori_loop(0, n_rows // rows_per_strip, step, 0, unroll=unroll)


def _inproj_kernel(x_ref, g_ref, w_ref, b_ref, wg_ref, bg_ref, wgt_ref, bgt_ref,
                   cw_ref, cb_ref, cos_ref, sin_ref,
                   z_ref, gc_ref, gr_ref, h_scr, carry_scr, *, tiles_per_seq):
    i = pl.program_id(0)
    j = pl.program_id(1)

    @pl.when(j == 0)
    def _():
        g = g_ref[...]

        def norm(rows):
            x = x_ref[rows, :]
            ms = jnp.mean(x * x, axis=-1, keepdims=True)
            h_scr[rows, :] = (x * lax.rsqrt(ms + EPS) * g).astype(BF16)
        _strips(TM_IN, norm, unroll=8)
        h = h_scr[...]
        gc_ref[...] = jnp.dot(h, wg_ref[...], preferred_element_type=F32) + bg_ref[...]
        gr_ref[...] = lax.dot_general(wgt_ref[...], h, (((1,), (1,)), ((), ())),
                                      preferred_element_type=F32) + bgt_ref[...]

    def run(epilogue, epi_cols=EPI_COLS):
        for r in range(TM_IN // SUB_IN):
            rows = pl.ds(r * SUB_IN, SUB_IN)
            zb = jnp.dot(h_scr[rows, :], w_ref[...], preferred_element_type=F32) + b_ref[...]
            for c0 in range(0, TN_IN, epi_cols):
                cols = slice(c0, c0 + epi_cols)
                z_ref[rows, cols] = epilogue(zb[:, cols], rows, cols).astype(BF16)

    is_conv = j < Z_MV // TN_IN
    is_rope = (j == Z_RQ // TN_IN) | (j == Z_RK // TN_IN)
    is_sigmoid = (j == Z_MO // TN_IN) | (j >= Z_GM // TN_IN)
    is_swish = (j >= Z_RG // TN_IN) & (j < Z_GM // TN_IN)
    is_plain = jnp.logical_not(is_conv | is_rope | is_sigmoid | is_swish)

    @pl.when(is_plain)
    def _():
        z = jnp.dot(h_scr[...], w_ref[...], preferred_element_type=F32) + b_ref[...]
        z_ref[...] = z.astype(BF16)

    @pl.when(is_sigmoid)
    def _():
        run(lambda zc, rows, cols: _sigmoid(zc))

    @pl.when(is_swish)
    def _():
        run(lambda zc, rows, cols: zc * _sigmoid(zc))

    @pl.when(is_rope)
    def _():
        scale = jnp.where(j == Z_RK // TN_IN, R_DK ** -0.5, 1.0)

        def rope(zc, rows, cols):
            c = cos_ref[rows, :] * scale
            s = sin_ref[rows, :] * scale
            x1, x2 = zc[:, :LANE], zc[:, LANE:]
            return jnp.concatenate([x1 * c - x2 * s, x1 * s + x2 * c], axis=1)
        run(rope)

    @pl.when(is_conv)
    def _():
        @pl.when(i % tiles_per_seq == 0)
        def _():
            carry_scr[j] = jnp.zeros(carry_scr.shape[1:], F32)
        scale = jnp.where(j == Z_MQ // TN_IN, M_DK ** -0.5, 1.0)

        def conv(zc, rows, cols):
            ext = jnp.concatenate([carry_scr[j, :, cols], zc], axis=0)
            carry_scr[j, :, cols] = zc[SUB_IN - 8:, :]
            acc = ext * cw_ref[0:1, cols]
            for tap in range(1, CONV_K):
                acc = ext * cw_ref[tap:tap + 1, cols] + pltpu.roll(acc, 1, axis=0)
            y = acc[8:, :] + cb_ref[:, cols]
            return y * _sigmoid(y) * scale
        run(conv, epi_cols=CONV_COLS)


def _inproj(layer, seq, x2d, norm_g, w_z, b_z, w_g, b_g, w_gt, b_gt, conv_w, conv_b, cos, sin):
    T = x2d.shape[0]
    tiles_per_seq = seq // TM_IN
    n_conv = Z_MV // TN_IN
    grid = (T // TM_IN, Z_COLS // TN_IN)
    return pl.pallas_call(
        functools.partial(_inproj_kernel, tiles_per_seq=tiles_per_seq),
        grid=grid,
        in_specs=[
            pl.BlockSpec((TM_IN, D_MODEL), lambda i, j: (i, 0)),
            pl.BlockSpec((None, 1, D_MODEL), lambda i, j: (layer, 0, 0)),
            pl.BlockSpec((None, D_MODEL, TN_IN), lambda i, j: (layer, 0, j)),
            pl.BlockSpec((None, 1, TN_IN), lambda i, j: (layer, 0, j)),
            pl.BlockSpec((None, D_MODEL, GATE_LANES), lambda i, j: (layer, 0, 0)),
            pl.BlockSpec((None, 1, GATE_LANES), lambda i, j: (layer, 0, 0)),
            pl.BlockSpec((None, GATE_ROWS, D_MODEL), lambda i, j: (layer, 0, 0)),
            pl.BlockSpec((None, GATE_ROWS, 1), lambda i, j: (layer, 0, 0)),
            pl.BlockSpec((None, CONV_K, TN_IN), lambda i, j: (layer, 0, jnp.minimum(j, n_conv - 1))),
            pl.BlockSpec((None, 1, TN_IN), lambda i, j: (layer, 0, jnp.minimum(j, n_conv - 1))),
            pl.BlockSpec((TM_IN, LANE), lambda i, j: (i % tiles_per_seq, 0)),
            pl.BlockSpec((TM_IN, LANE), lambda i, j: (i % tiles_per_seq, 0)),
        ],
        out_specs=[
            pl.BlockSpec((TM_IN, TN_IN), lambda i, j: (i, j)),
            pl.BlockSpec((TM_IN, GATE_LANES), lambda i, j: (i, 0)),
            pl.BlockSpec((GATE_ROWS, TM_IN), lambda i, j: (0, i)),
        ],
        out_shape=[
            jax.ShapeDtypeStruct((T, Z_COLS), BF16),
            jax.ShapeDtypeStruct((T, GATE_LANES), F32),
            jax.ShapeDtypeStruct((GATE_ROWS, T), F32),
        ],
        scratch_shapes=[
            pltpu.VMEM((TM_IN, D_MODEL), BF16),
            pltpu.VMEM((n_conv, 8, TN_IN), F32),
        ],
        compiler_params=pltpu.CompilerParams(
            dimension_semantics=("arbitrary", "arbitrary"),
            vmem_limit_bytes=VMEM_LIMIT),
        name="inproj",
    )(x2d, norm_g, w_z, b_z, w_g, b_g, w_gt, b_gt, conv_w, conv_b, cos, sin)


def _mixer_kernel(q_ref, k_ref, v_ref, o_ref, rq_ref, rk_ref, rv_ref, rg_ref,
                  gc_ref, gr_ref, intra_ref, rcol_ref, mg_ref, rgain_ref,
                  ym_ref, yr_ref,
                  hm_scr, hr_scr, c_scr, r_scr, m_scr, *, cdec):
    L = CHUNK_L
    TS = TS_MIX
    t = pl.program_id(1)

    @pl.when(t == 0)
    def _():
        c_scr[...] = jnp.zeros(c_scr.shape, F32)
        r_scr[...] = jnp.zeros(r_scr.shape, F32)
        m_scr[...] = jnp.zeros(m_scr.shape, F32)

    row_id = lax.broadcasted_iota(jnp.int32, (L, L), 0)
    col_id = lax.broadcasted_iota(jnp.int32, (L, L), 1)
    causal = row_id >= col_id
    tril = causal.astype(BF16)
    triu = (row_id <= col_id).astype(BF16)
    ones_pad = jnp.ones((L, LANE), BF16)
    rcol = rcol_ref[...]

    for c in range(TS // L):
        rows = slice(c * L, (c + 1) * L)
        gcol = gc_ref[rows, :]
        grow = gr_ref[:, rows]
        b_cols = _dot01_left(tril, _log_sigmoid(gcol))
        b_rows = _dot01_right(_log_sigmoid(grow), triu)
        for h in range(HEADS):
            sl = slice(h * M_DK, (h + 1) * M_DK)
            q = q_ref[rows, sl]
            k = k_ref[rows, sl]
            vaug = jnp.concatenate([v_ref[rows, sl], ones_pad], axis=1)
            bcol = b_cols[:, HEADS + h:HEADS + h + 1]
            licol = gcol[:, h:h + 1]
            brow = b_rows[HEADS + h:HEADS + h + 1, :]
            lirow = grow[h:h + 1, :]
            m_prev = m_scr[h][0:1, 0:1]

            log_intra = jnp.where(causal, bcol - brow + lirow, -jnp.inf)
            inter = bcol + m_prev
            m_t = jnp.maximum(inter, jnp.max(log_intra, axis=-1, keepdims=True))
            dmat = jnp.exp(log_intra - m_t)
            d_inter = jnp.exp(inter - m_t)
            s = lax.dot_general(q, k, (((1,), (1,)), ((), ())), preferred_element_type=F32) * dmat
            c_old = c_scr[h]
            num = (jnp.dot(s.astype(BF16), vaug, preferred_element_type=F32)
                   + d_inter * jnp.dot(q, c_old.astype(BF16), preferred_element_type=F32))
            den = num[:, M_DV:M_DV + 1]
            scale = 1.0 / jnp.maximum(jnp.abs(den), jnp.exp(-m_t))
            hm_scr[rows, sl] = num[:, :M_DV] * scale

            b_end = bcol[L - 1:L, :]
            log_w = b_end - bcol + licol
            m_new = jnp.maximum(b_end + m_prev, jnp.max(log_w, axis=0, keepdims=True))
            w = jnp.exp(log_w - m_new)
            decay = jnp.exp(b_end + m_prev - m_new)
            kw = (k.astype(F32) * w).astype(BF16)
            c_scr[h] = decay * c_old + lax.dot_general(kw, vaug, (((0,), (0,)), ((), ())),
                                                       preferred_element_type=F32)
            m_scr[h] = jnp.broadcast_to(m_new, m_scr.shape[1:])

            sl = slice(h * R_DK, (h + 1) * R_DK)
            vs = slice(h * R_DV, (h + 1) * R_DV)
            rq = rq_ref[rows, sl]
            rk = rk_ref[rows, sl]
            rv = rv_ref[rows, vs]
            s = lax.dot_general(rq, rk, (((1,), (1,)), ((), ())), preferred_element_type=F32) * intra_ref[h]
            r_old = r_scr[h]
            hr_scr[rows, vs] = (jnp.dot(s.astype(BF16), rv, preferred_element_type=F32)
                                + jnp.dot(rq, r_old.astype(BF16), preferred_element_type=F32)
                                * rcol[:, h:h + 1])
            rkw = (rk.astype(F32) * rcol[:, HEADS + h:HEADS + h + 1]).astype(BF16)
            r_scr[h] = cdec[h] * r_old + lax.dot_general(rkw, rv, (((0,), (0,)), ((), ())),
                                                         preferred_element_type=F32)

    mg = mg_ref[...]
    rgain = rgain_ref[...]

    def head_norm(x):
        mu = jnp.mean(x, axis=-1, keepdims=True)
        xc = x - mu
        var = jnp.mean(xc * xc, axis=-1, keepdims=True)
        return xc * lax.rsqrt(var + EPS)

    def finish(rows):
        for h in range(HEADS):
            sl = slice(h * M_DV, (h + 1) * M_DV)
            y = o_ref[rows, sl].astype(F32) * (head_norm(hm_scr[rows, sl]) * mg[:, sl])
            ym_ref[rows, sl] = y.astype(BF16)
            vs = slice(h * R_DV, (h + 1) * R_DV)
            y = rg_ref[rows, vs].astype(F32) * (head_norm(hr_scr[rows, vs]) * rgain[:, vs])
            yr_ref[rows, vs] = y.astype(BF16)
    _strips(TS, finish, unroll=2)


def _retention_tables(L):
    gamma = 1.0 - np.exp2(-5.0 - np.arange(HEADS, dtype=np.float64))
    lg = np.log(gamma)
    pos = np.arange(L, dtype=np.float64)
    diff = pos[:, None] - pos[None, :]
    intra = np.where(diff >= 0, np.exp(np.maximum(diff, 0.0) * lg[:, None, None]), 0.0)
    rcol = np.zeros((L, LANE), np.float64)
    rcol[:, :HEADS] = np.exp((pos[:, None] + 1.0) * lg[None, :])
    rcol[:, HEADS:2 * HEADS] = np.exp((L - 1.0 - pos[:, None]) * lg[None, :])
    cdec = tuple(float(v) for v in np.exp(L * lg))
    return jnp.asarray(intra, F32), jnp.asarray(rcol, F32), cdec


def _mixer(layer, batch, seq, z, gcol, grow, m_gain, r_gain):
    L = CHUNK_L
    TS = TS_MIX
    nt = seq // TS
    intra, rcol, cdec = _retention_tables(L)

    def zspec(width, col0):
        blk = col0 // width
        return pl.BlockSpec((TS, width), lambda b, t: (b * nt + t, blk))

    def const(shape):
        return pl.BlockSpec(shape, lambda b, t: tuple(0 for _ in shape))

    def per_layer(width):
        return pl.BlockSpec((None, 1, width), lambda b, t: (layer, 0, 0))

    T = batch * seq
    return pl.pallas_call(
        functools.partial(_mixer_kernel, cdec=cdec),
        grid=(batch, nt),
        in_specs=[
            zspec(M_QK, Z_MQ), zspec(M_QK, Z_MK), zspec(M_V, Z_MV), zspec(M_V, Z_MO),
            zspec(R_QK, Z_RQ), zspec(R_QK, Z_RK), zspec(R_V, Z_RV), zspec(R_V, Z_RG),
            pl.BlockSpec((TS, GATE_LANES), lambda b, t: (b * nt + t, 0)),
            pl.BlockSpec((GATE_ROWS, TS), lambda b, t: (0, b * nt + t)),
            const((HEADS, L, L)), const((L, LANE)),
            per_layer(M_V), per_layer(R_V),
        ],
        out_specs=[
            pl.BlockSpec((TS, M_V), lambda b, t: (b * nt + t, 0)),
            pl.BlockSpec((TS, R_V), lambda b, t: (b * nt + t, 0)),
        ],
        out_shape=[
            jax.ShapeDtypeStruct((T, M_V), BF16),
            jax.ShapeDtypeStruct((T, R_V), BF16),
        ],
        scratch_shapes=[
            pltpu.VMEM((TS, M_V), F32), pltpu.VMEM((TS, R_V), F32),
            pltpu.VMEM((HEADS, M_DK, M_DV + LANE), F32),
            pltpu.VMEM((HEADS, R_DK, R_DV), F32),
            pltpu.VMEM((HEADS, 8, LANE), F32),
        ],
        compiler_params=pltpu.CompilerParams(
            dimension_semantics=("arbitrary", "arbitrary"),
            vmem_limit_bytes=VMEM_LIMIT),
        name="mixer",
    )(z, z, z, z, z, z, z, z, gcol, grow, intra, rcol, m_gain, r_gain)


def _rmsnorm_rows(src_ref, dst_ref, g, n_rows):
    def norm(rows):
        x = src_ref[rows, :]
        ms = jnp.mean(x * x, axis=-1, keepdims=True)
        dst_ref[rows, :] = (x * lax.rsqrt(ms + EPS) * g).astype(dst_ref.dtype)
    _strips(n_rows, norm, unroll=8)


def _post_kernel(ym_ref, yr_ref, gm_ref, gr_ref, x_ref, p_ref,
                 wbm_ref, wbr_ref, wout_ref, g2_ref, w1_ref, b1_ref, w2_ref, b2_ref,
                 g3_ref, wpg_ref, wpe_ref, gf_ref,
                 out_ref, x_scr, h_scr, *, final):
    a = jnp.dot(ym_ref[...], wbm_ref[...], preferred_element_type=F32)
    merged = gm_ref[...].astype(F32) * a
    b = jnp.dot(yr_ref[...], wbr_ref[...], preferred_element_type=F32)
    merged = merged + gr_ref[...].astype(F32) * b
    x_scr[...] = x_ref[...] + jnp.dot(merged.astype(BF16), wout_ref[...], preferred_element_type=F32)

    _rmsnorm_rows(x_scr, h_scr, g2_ref[...], TM_POST)
    h = h_scr[...]
    acc = x_scr[...] + b2_ref[...]
    for c in range(D_FF // D_MODEL):
        cs = slice(c * D_MODEL, (c + 1) * D_MODEL)
        u = jnp.maximum(jnp.dot(h, w1_ref[:, cs], preferred_element_type=F32) + b1_ref[:, cs], 0.0)
        acc = acc + jnp.dot((u * u).astype(BF16), w2_ref[cs, :], preferred_element_type=F32)
    x_scr[...] = acc

    _rmsnorm_rows(x_scr, h_scr, g3_ref[...], TM_POST)
    gate = _sigmoid(jnp.dot(h_scr[...], wpg_ref[...], preferred_element_type=F32))
    pe = jnp.dot(p_ref[...].astype(BF16), wpe_ref[...], preferred_element_type=F32)
    if final:
        x_scr[...] = x_scr[...] + gate * pe
        _rmsnorm_rows(x_scr, out_ref, gf_ref[...], TM_POST)
    else:
        out_ref[...] = x_scr[...] + gate * pe


def _post(layer, final, ym, yr, z, x2d, p3d, w_bm, w_br, w_out, g2, w1, b1, w2, b2, g3, wpg, wpe, gf):
    T = x2d.shape[0]
    tm = TM_POST

    def rows(width, blk=0):
        return pl.BlockSpec((tm, width), lambda i: (i, blk))

    def weight(k, n):
        return pl.BlockSpec((None, k, n), lambda i: (layer, 0, 0), pipeline_mode=pl.Buffered(1))

    def vec(n):
        return pl.BlockSpec((None, 1, n), lambda i: (layer, 0, 0))

    return pl.pallas_call(
        functools.partial(_post_kernel, final=final),
        grid=(T // tm,),
        in_specs=[
            rows(M_V), rows(R_V), rows(D_MODEL, Z_GM // D_MODEL), rows(D_MODEL, Z_GR // D_MODEL),
            rows(D_MODEL),
            pl.BlockSpec((None, tm, PE_DIM), lambda i: (layer, i, 0)),
            weight(M_V, D_MODEL), weight(R_V, D_MODEL), weight(D_MODEL, D_MODEL),
            vec(D_MODEL), weight(D_MODEL, D_FF), vec(D_FF), weight(D_FF, D_MODEL), vec(D_MODEL),
            vec(D_MODEL), weight(D_MODEL, D_MODEL), weight(PE_DIM, D_MODEL),
            pl.BlockSpec((1, D_MODEL), lambda i: (0, 0)),
        ],
        out_specs=rows(D_MODEL),
        out_shape=jax.ShapeDtypeStruct((T, D_MODEL), F32),
        scratch_shapes=[pltpu.VMEM((tm, D_MODEL), F32), pltpu.VMEM((tm, D_MODEL), BF16)],
        compiler_params=pltpu.CompilerParams(
            dimension_semantics=("arbitrary",),
            vmem_limit_bytes=VMEM_LIMIT),
        name="post_final" if final else "post",
    )(ym, yr, z, z, x2d, p3d, w_bm, w_br, w_out, g2, w1, b1, w2, b2, g3, wpg, wpe, gf)


def _reorder_in_proj(w_in, b_in):
    big = [(0, 4096), (4104, 4104 + 6144 + 2048)]
    w_z = jnp.concatenate([w_in[..., a:b] for a, b in big], axis=-1)
    b_z = jnp.concatenate([b_in[..., a:b] for a, b in big], axis=-1)
    w_gate = w_in[..., 4096:4104]
    b_gate = b_in[..., 4096:4104]
    return w_z, b_z, w_gate, b_gate


def kernel(x, p, norm1_g, w_in, b_in, conv_w, conv_b, m_norm_g, r_norm_g, w_bm, w_br, w_out,
           norm2_g, w_ff1, b_ff1, w_ff2, b_ff2, norm3_g, w_pe_gate, w_pe, final_g):
    B, S, _ = x.shape
    depth = w_in.shape[0]
    T = B * S
    assert S % TS_MIX == 0 and TS_MIX % CHUNK_L == 0 and S % TM_IN == 0 and T % TM_POST == 0

    w_z, b_z, w_gate, b_gate = _reorder_in_proj(w_in, b_in)
    w_z = w_z.astype(BF16)
    b_z = b_z[:, None, :]
    pad = GATE_LANES - GATE_ROWS
    w_g = jnp.pad(w_gate, ((0, 0), (0, 0), (0, pad))).astype(BF16)
    b_g = jnp.pad(b_gate, ((0, 0), (0, pad)))[:, None, :]
    w_gt = jnp.swapaxes(w_gate, 1, 2).astype(BF16)
    b_gt = b_gate[:, :, None]

    pos = jnp.arange(S, dtype=F32)
    inv_freq = ROPE_BASE ** (-jnp.arange(0, R_DK, 2, dtype=F32) / R_DK)
    ang = pos[:, None] * inv_freq[None, :]
    cos = jnp.cos(ang)
    sin = jnp.sin(ang)

    bf = lambda w: w.astype(BF16)
    row = lambda v: v[:, None, :]
    w_bm, w_br, w_out, w_ff1, w_ff2, w_pe_gate, w_pe = map(
        bf, (w_bm, w_br, w_out, w_ff1, w_ff2, w_pe_gate, w_pe))
    norm1_g, conv_b, m_norm_g, r_norm_g, norm2_g, b_ff1, b_ff2, norm3_g = map(
        row, (norm1_g, conv_b, m_norm_g, r_norm_g, norm2_g, b_ff1, b_ff2, norm3_g))
    final_g = final_g[None, :]

    x2d = x.reshape(T, D_MODEL)
    p3d = p.reshape(depth, T, PE_DIM)
    for i in range(depth):
        z, gcol, grow = _inproj(i, S, x2d, norm1_g, w_z, b_z, w_g, b_g, w_gt, b_gt,
                                conv_w, conv_b, cos, sin)
        ym, yr = _mixer(i, B, S, z, gcol, grow, m_norm_g, r_norm_g)
        x2d = _post(i, i == depth - 1, ym, yr, z, x2d, p3d, w_bm, w_br, w_out,
                    norm2_g, w_ff1, b_ff1, w_ff2, b_ff2, norm3_g, w_pe_gate, w_pe, final_g)
    return x2d.reshape(B, S, D_MODEL)
```

```python
import functools

import numpy as np
import jax
import jax.numpy as jnp
from jax import lax
from jax.experimental import pallas as pl
from jax.experimental.pallas import tpu as pltpu

F32 = jnp.float32
BF16 = jnp.bfloat16

D_MODEL = 1024
PE_DIM = 256
EPS = 1e-6
HEADS = 4
M_DK = 256
M_DV = 256
R_DK = 256
R_DV = 512
M_QK = HEADS * M_DK
M_V = HEADS * M_DV
R_QK = HEADS * R_DK
R_V = HEADS * R_DV
CONV_K = 4
ROPE_BASE = 10000.0
D_FF = 4 * D_MODEL

Z_MQ, Z_MK, Z_MV, Z_MO = 0, 1024, 2048, 3072
Z_RQ, Z_RK, Z_RV, Z_RG = 4096, 5120, 6144, 8192
Z_GM, Z_GR = 10240, 11264
Z_COLS = 12288
GATE_LANES = 128
GATE_ROWS = 8

LANE = 128
VMEM_LIMIT = 60 * 1024 * 1024

TM_IN = 2048
TN_IN = 1024
SUB_ROWS = (512, 512, 512, 512)
EPI_COLS = 256
CONV_COLS = 512
CHUNK_L = 256
TS_MIX = 512
TM_POST = 256
STRIP = 16


def _sigmoid(x):
    return 0.5 * jnp.tanh(0.5 * x) + 0.5


def _swish(x):
    u = 0.5 * x
    return u * jnp.tanh(u) + u


def _log_sigmoid(x):
    return jnp.minimum(x, 0.0) - jnp.log1p(jnp.exp(-jnp.abs(x)))


def _split3(x):
    hi = x.astype(BF16)
    r1 = x - hi.astype(F32)
    mid = r1.astype(BF16)
    lo = (r1 - mid.astype(F32)).astype(BF16)
    return hi, mid, lo


def _dot01_left(mat01, x):
    return sum(jnp.dot(mat01, piece, preferred_element_type=F32) for piece in _split3(x))


def _dot01_right(x, mat01):
    return sum(jnp.dot(piece, mat01, preferred_element_type=F32) for piece in _split3(x))


def _strips(n_rows, body, rows_per_strip=STRIP, unroll=1):
    def step(i, carry):
        body(pl.ds(pl.multiple_of(i * rows_per_strip, rows_per_strip), rows_per_strip))
        return carry
    lax.fori_loop(0, n_rows // rows_per_strip, step, 0, unroll=unroll)


def _inproj_kernel(x_ref, g_ref, w_ref, b_ref, wg_ref, bg_ref,
                   cw_ref, cb_ref, cos_ref, sin_ref,
                   z_ref, gc_ref, gr_ref, h_scr, carry_scr, *, tiles_per_seq):
    i = pl.program_id(0)
    j = pl.program_id(1)

    @pl.when(j == 0)
    def _():
        g = g_ref[...]

        def norm(rows):
            x = x_ref[rows, :]
            ms = jnp.mean(x * x, axis=-1, keepdims=True)
            h_scr[rows, :] = (x * lax.rsqrt(ms + EPS) * g).astype(BF16)
        _strips(TM_IN, norm, unroll=8)
        h = h_scr[...]
        gates = jnp.dot(h, wg_ref[...], preferred_element_type=F32) + bg_ref[...]
        gc_ref[...] = gates
        gr_ref[...] = gates.T[:GATE_ROWS, :]

    def run(epilogue, epi_cols=EPI_COLS):
        assert sum(SUB_ROWS) == TM_IN
        for r0, n in zip(np.cumsum((0,) + SUB_ROWS[:-1]), SUB_ROWS):
            rows = pl.ds(int(r0), n)
            zb = jnp.dot(h_scr[rows, :], w_ref[...], preferred_element_type=F32) + b_ref[...]
            for c0 in range(0, TN_IN, epi_cols):
                cols = slice(c0, c0 + epi_cols)
                z_ref[rows, cols] = epilogue(zb[:, cols], rows, cols).astype(BF16)

    is_conv = j < Z_MV // TN_IN
    is_rope = (j == Z_RQ // TN_IN) | (j == Z_RK // TN_IN)
    is_sigmoid = (j == Z_MO // TN_IN) | (j >= Z_GM // TN_IN)
    is_swish = (j >= Z_RG // TN_IN) & (j < Z_GM // TN_IN)
    is_plain = jnp.logical_not(is_conv | is_rope | is_sigmoid | is_swish)

    @pl.when(is_plain)
    def _():
        z = jnp.dot(h_scr[...], w_ref[...], preferred_element_type=F32) + b_ref[...]
        z_ref[...] = z.astype(BF16)

    @pl.when(is_sigmoid)
    def _():
        run(lambda zc, rows, cols: _sigmoid(zc))

    @pl.when(is_swish)
    def _():
        run(lambda zc, rows, cols: _swish(zc))

    @pl.when(is_rope)
    def _():
        scale = jnp.where(j == Z_RK // TN_IN, R_DK ** -0.5, 1.0)

        def rope(zc, rows, cols):
            c = cos_ref[rows, :] * scale
            s = sin_ref[rows, :] * scale
            x1, x2 = zc[:, :LANE], zc[:, LANE:]
            return jnp.concatenate([x1 * c - x2 * s, x1 * s + x2 * c], axis=1)
        run(rope)

    @pl.when(is_conv)
    def _():
        @pl.when(i % tiles_per_seq == 0)
        def _():
            carry_scr[j] = jnp.zeros(carry_scr.shape[1:], F32)
        scale = jnp.where(j == Z_MQ // TN_IN, M_DK ** -0.5, 1.0)

        row7 = lax.broadcasted_iota(jnp.int32, (8, CONV_COLS), 0) == 7

        def conv(zc, rows, cols):
            pieces = [carry_scr[j, :, cols]] + [zc[p:p + 8, :] for p in range(0, zc.shape[0], 8)]
            carry_scr[j, :, cols] = pieces[-1]
            acc = [x * cw_ref[0:1, cols] for x in pieces]
            for tap in range(1, CONV_K):
                w_tap = cw_ref[tap:tap + 1, cols]
                mixed = [acc[0]] + [jnp.where(row7, acc[p - 1], acc[p]) for p in range(1, len(acc))]
                acc = [x * w_tap + pltpu.roll(m, 1, axis=0) for x, m in zip(pieces, mixed)]
            y = jnp.concatenate(acc[1:], axis=0) + cb_ref[:, cols]
            return _swish(y) * scale
        run(conv, epi_cols=CONV_COLS)


def _inproj(layer, seq, x2d, norm_g, w_z, b_z, w_g, b_g, conv_w, conv_b, cos, sin):
    T = x2d.shape[0]
    tiles_per_seq = seq // TM_IN
    n_conv = Z_MV // TN_IN
    grid = (T // TM_IN, Z_COLS // TN_IN)
    return pl.pallas_call(
        functools.partial(_inproj_kernel, tiles_per_seq=tiles_per_seq),
        grid=grid,
        in_specs=[
            pl.BlockSpec((TM_IN, D_MODEL), lambda i, j: (i, 0)),
            pl.BlockSpec((None, 1, D_MODEL), lambda i, j: (layer, 0, 0)),
            pl.BlockSpec((None, D_MODEL, TN_IN), lambda i, j: (layer, 0, j)),
            pl.BlockSpec((None, 1, TN_IN), lambda i, j: (layer, 0, j)),
            pl.BlockSpec((None, D_MODEL, GATE_LANES), lambda i, j: (layer, 0, 0)),
            pl.BlockSpec((None, 1, GATE_LANES), lambda i, j: (layer, 0, 0)),
            pl.BlockSpec((None, CONV_K, TN_IN), lambda i, j: (layer, 0, jnp.minimum(j, n_conv - 1))),
            pl.BlockSpec((None, 1, TN_IN), lambda i, j: (layer, 0, jnp.minimum(j, n_conv - 1))),
            pl.BlockSpec((TM_IN, LANE), lambda i, j: (i % tiles_per_seq, 0)),
            pl.BlockSpec((TM_IN, LANE), lambda i, j: (i % tiles_per_seq, 0)),
        ],
        out_specs=[
            pl.BlockSpec((TM_IN, TN_IN), lambda i, j: (i, j)),
            pl.BlockSpec((TM_IN, GATE_LANES), lambda i, j: (i, 0)),
            pl.BlockSpec((GATE_ROWS, TM_IN), lambda i, j: (0, i)),
        ],
        out_shape=[
            jax.ShapeDtypeStruct((T, Z_COLS), BF16),
            jax.ShapeDtypeStruct((T, GATE_LANES), F32),
            jax.ShapeDtypeStruct((GATE_ROWS, T), F32),
        ],
        scratch_shapes=[
            pltpu.VMEM((TM_IN, D_MODEL), BF16),
            pltpu.VMEM((n_conv, 8, TN_IN), F32),
        ],
        compiler_params=pltpu.CompilerParams(
            dimension_semantics=("arbitrary", "arbitrary"),
            vmem_limit_bytes=VMEM_LIMIT),
        name="inproj",
    )(x2d, norm_g, w_z, b_z, w_g, b_g, conv_w, conv_b, cos, sin)


def _mixer_kernel(q_ref, k_ref, v_ref, o_ref, rq_ref, rk_ref, rv_ref, rg_ref,
                  gc_ref, gr_ref, intra_ref, rcol_ref, mg_ref, rgain_ref,
                  ym_ref, yr_ref,
                  hm_scr, hr_scr, c_scr, r_scr, m_scr, *, cdec):
    L = CHUNK_L
    TS = TS_MIX
    t = pl.program_id(1)

    @pl.when(t == 0)
    def _():
        c_scr[...] = jnp.zeros(c_scr.shape, F32)
        r_scr[...] = jnp.zeros(r_scr.shape, F32)
        m_scr[...] = jnp.zeros(m_scr.shape, F32)

    row_id = lax.broadcasted_iota(jnp.int32, (L, L), 0)
    col_id = lax.broadcasted_iota(jnp.int32, (L, L), 1)
    causal = row_id >= col_id
    tril = causal.astype(BF16)
    triu = (row_id <= col_id).astype(BF16)
    ones_pad = jnp.ones((L, LANE), BF16)
    rcol = rcol_ref[...]

    for c in range(TS // L):
        rows = slice(c * L, (c + 1) * L)
        gcol = gc_ref[rows, :]
        grow = gr_ref[:, rows]
        b_cols = _dot01_left(tril, _log_sigmoid(gcol))
        b_rows = _dot01_right(_log_sigmoid(grow), triu)
        for h in range(HEADS):
            sl = slice(h * M_DK, (h + 1) * M_DK)
            q = q_ref[rows, sl]
            k = k_ref[rows, sl]
            vaug = jnp.concatenate([v_ref[rows, sl], ones_pad], axis=1)
            bcol = b_cols[:, HEADS + h:HEADS + h + 1]
            licol = gcol[:, h:h + 1]
            brow = b_rows[HEADS + h:HEADS + h + 1, :]
            lirow = grow[h:h + 1, :]
            m_prev = m_scr[h][0:1, 0:1]

            log_intra = jnp.where(causal, bcol - brow + lirow, -jnp.inf)
            inter = bcol + m_prev
            m_t = jnp.maximum(inter, jnp.max(log_intra, axis=-1, keepdims=True))
            dmat = jnp.exp(log_intra - m_t)
            d_inter = jnp.exp(inter - m_t)
            s = lax.dot_general(q, k, (((1,), (1,)), ((), ())), preferred_element_type=F32) * dmat
            c_old = c_scr[h]
            num = (jnp.dot(s.astype(BF16), vaug, preferred_element_type=F32)
                   + d_inter * jnp.dot(q, c_old.astype(BF16), preferred_element_type=F32))
            den = num[:, M_DV:M_DV + 1]
            scale = 1.0 / jnp.maximum(jnp.abs(den), jnp.exp(-m_t))
            hm_scr[rows, sl] = num[:, :M_DV] * scale

            b_end = bcol[L - 1:L, :]
            log_w = b_end - bcol + licol
            m_new = jnp.maximum(b_end + m_prev, jnp.max(log_w, axis=0, keepdims=True))
            w = jnp.exp(log_w - m_new)
            decay = jnp.exp(b_end + m_prev - m_new)
            kw = (k.astype(F32) * w).astype(BF16)
            c_scr[h] = decay * c_old + lax.dot_general(kw, vaug, (((0,), (0,)), ((), ())),
                                                       preferred_element_type=F32)
            m_scr[h] = jnp.broadcast_to(m_new, m_scr.shape[1:])

            sl = slice(h * R_DK, (h + 1) * R_DK)
            vs = slice(h * R_DV, (h + 1) * R_DV)
            rq = rq_ref[rows, sl]
            rk = rk_ref[rows, sl]
            rv = rv_ref[rows, vs]
            s = lax.dot_general(rq, rk, (((1,), (1,)), ((), ())), preferred_element_type=F32) * intra_ref[h]
            r_old = r_scr[h]
            hr_scr[rows, vs] = (jnp.dot(s.astype(BF16), rv, preferred_element_type=F32)
                                + jnp.dot(rq, r_old.astype(BF16), preferred_element_type=F32)
                                * rcol[:, h:h + 1])
            rkw = (rk.astype(F32) * rcol[:, HEADS + h:HEADS + h + 1]).astype(BF16)
            r_scr[h] = cdec[h] * r_old + lax.dot_general(rkw, rv, (((0,), (0,)), ((), ())),
                                                         preferred_element_type=F32)

    mg = mg_ref[...]
    rgain = rgain_ref[...]

    def head_norm(x):
        mu = jnp.mean(x, axis=-1, keepdims=True)
        xc = x - mu
        var = jnp.mean(xc * xc, axis=-1, keepdims=True)
        return xc * lax.rsqrt(var + EPS)

    def finish(rows):
        for h in range(HEADS):
            sl = slice(h * M_DV, (h + 1) * M_DV)
            y = o_ref[rows, sl].astype(F32) * (head_norm(hm_scr[rows, sl]) * mg[:, sl])
            ym_ref[rows, sl] = y.astype(BF16)
            vs = slice(h * R_DV, (h + 1) * R_DV)
            y = rg_ref[rows, vs].astype(F32) * (head_norm(hr_scr[rows, vs]) * rgain[:, vs])
            yr_ref[rows, vs] = y.astype(BF16)
    _strips(TS, finish, unroll=4)


def _retention_tables(L):
    gamma = 1.0 - np.exp2(-5.0 - np.arange(HEADS, dtype=np.float64))
    lg = np.log(gamma)
    pos = np.arange(L, dtype=np.float64)
    diff = pos[:, None] - pos[None, :]
    intra = np.where(diff >= 0, np.exp(np.maximum(diff, 0.0) * lg[:, None, None]), 0.0)
    rcol = np.zeros((L, LANE), np.float64)
    rcol[:, :HEADS] = np.exp((pos[:, None] + 1.0) * lg[None, :])
    rcol[:, HEADS:2 * HEADS] = np.exp((L - 1.0 - pos[:, None]) * lg[None, :])
    cdec = tuple(float(v) for v in np.exp(L * lg))
    return jnp.asarray(intra, F32), jnp.asarray(rcol, F32), cdec


def _mixer(layer, batch, seq, z, gcol, grow, m_gain, r_gain):
    L = CHUNK_L
    TS = TS_MIX
    nt = seq // TS
    intra, rcol, cdec = _retention_tables(L)

    def zspec(width, col0):
        blk = col0 // width
        return pl.BlockSpec((TS, width), lambda b, t: (b * nt + t, blk))

    def const(shape):
        return pl.BlockSpec(shape, lambda b, t: tuple(0 for _ in shape))

    def per_layer(width):
        return pl.BlockSpec((None, 1, width), lambda b, t: (layer, 0, 0))

    T = batch * seq
    return pl.pallas_call(
        functools.partial(_mixer_kernel, cdec=cdec),
        grid=(batch, nt),
        in_specs=[
            zspec(M_QK, Z_MQ), zspec(M_QK, Z_MK), zspec(M_V, Z_MV), zspec(M_V, Z_MO),
            zspec(R_QK, Z_RQ), zspec(R_QK, Z_RK), zspec(R_V, Z_RV), zspec(R_V, Z_RG),
            pl.BlockSpec((TS, GATE_LANES), lambda b, t: (b * nt + t, 0)),
            pl.BlockSpec((GATE_ROWS, TS), lambda b, t: (0, b * nt + t)),
            const((HEADS, L, L)), const((L, LANE)),
            per_layer(M_V), per_layer(R_V),
        ],
        out_specs=[
            pl.BlockSpec((TS, M_V), lambda b, t: (b * nt + t, 0)),
            pl.BlockSpec((TS, R_V), lambda b, t: (b * nt + t, 0)),
        ],
        out_shape=[
            jax.ShapeDtypeStruct((T, M_V), BF16),
            jax.ShapeDtypeStruct((T, R_V), BF16),
        ],
        scratch_shapes=[
            pltpu.VMEM((TS, M_V), F32), pltpu.VMEM((TS, R_V), F32),
            pltpu.VMEM((HEADS, M_DK, M_DV + LANE), F32),
            pltpu.VMEM((HEADS, R_DK, R_DV), F32),
            pltpu.VMEM((HEADS, 8, LANE), F32),
        ],
        compiler_params=pltpu.CompilerParams(
            dimension_semantics=("arbitrary", "arbitrary"),
            vmem_limit_bytes=VMEM_LIMIT),
        name="mixer",
    )(z, z, z, z, z, z, z, z, gcol, grow, intra, rcol, m_gain, r_gain)


def _rmsnorm_rows(src_ref, dst_ref, g, n_rows):
    def norm(rows):
        x = src_ref[rows, :]
        ms = jnp.mean(x * x, axis=-1, keepdims=True)
        dst_ref[rows, :] = (x * lax.rsqrt(ms + EPS) * g).astype(dst_ref.dtype)
    _strips(n_rows, norm, unroll=8)


def _post_kernel(ym_ref, yr_ref, gm_ref, gr_ref, x_ref, p_ref,
                 wbm_ref, wbr_ref, wout_ref, g2_ref, w1_ref, b1_ref, w2_ref, b2_ref,
                 g3_ref, wpg_ref, wpe_ref, gf_ref,
                 out_ref, x_scr, h_scr, *, final):
    a = jnp.dot(ym_ref[...], wbm_ref[...], preferred_element_type=F32)
    merged = gm_ref[...].astype(F32) * a
    b = jnp.dot(yr_ref[...], wbr_ref[...], preferred_element_type=F32)
    merged = merged + gr_ref[...].astype(F32) * b
    x_scr[...] = x_ref[...] + jnp.dot(merged.astype(BF16), wout_ref[...], preferred_element_type=F32)

    _rmsnorm_rows(x_scr, h_scr, g2_ref[...], TM_POST)
    h = h_scr[...]
    acc = x_scr[...] + b2_ref[...]
    for c in range(D_FF // D_MODEL):
        cs = slice(c * D_MODEL, (c + 1) * D_MODEL)
        u = jnp.maximum(jnp.dot(h, w1_ref[:, cs], preferred_element_type=F32) + b1_ref[:, cs], 0.0)
        acc = acc + jnp.dot((u * u).astype(BF16), w2_ref[cs, :], preferred_element_type=F32)
    x_scr[...] = acc

    _rmsnorm_rows(x_scr, h_scr, g3_ref[...], TM_POST)
    gate = _sigmoid(jnp.dot(h_scr[...], wpg_ref[...], preferred_element_type=F32))
    pe = jnp.dot(p_ref[...].astype(BF16), wpe_ref[...], preferred_element_type=F32)
    if final:
        x_scr[...] = x_scr[...] + gate * pe
        _rmsnorm_rows(x_scr, out_ref, gf_ref[...], TM_POST)
    else:
        out_ref[...] = x_scr[...] + gate * pe


def _post(layer, final, ym, yr, z, x2d, p3d, w_bm, w_br, w_out, g2, w1, b1, w2, b2, g3, wpg, wpe, gf):
    T = x2d.shape[0]
    tm = TM_POST

    def rows(width, blk=0):
        return pl.BlockSpec((tm, width), lambda i: (i, blk))

    def weight(k, n):
        return pl.BlockSpec((None, k, n), lambda i: (layer, 0, 0), pipeline_mode=pl.Buffered(1))

    def vec(n):
        return pl.BlockSpec((None, 1, n), lambda i: (layer, 0, 0))

    return pl.pallas_call(
        functools.partial(_post_kernel, final=final),
        grid=(T // tm,),
        in_specs=[
            rows(M_V), rows(R_V), rows(D_MODEL, Z_GM // D_MODEL), rows(D_MODEL, Z_GR // D_MODEL),
            rows(D_MODEL),
            pl.BlockSpec((None, tm, PE_DIM), lambda i: (layer, i, 0)),
            weight(M_V, D_MODEL), weight(R_V, D_MODEL), weight(D_MODEL, D_MODEL),
            vec(D_MODEL), weight(D_MODEL, D_FF), vec(D_FF), weight(D_FF, D_MODEL), vec(D_MODEL),
            vec(D_MODEL), weight(D_MODEL, D_MODEL), weight(PE_DIM, D_MODEL),
            pl.BlockSpec((1, D_MODEL), lambda i: (0, 0)),
        ],
        out_specs=rows(D_MODEL),
        out_shape=jax.ShapeDtypeStruct((T, D_MODEL), F32),
        scratch_shapes=[pltpu.VMEM((tm, D_MODEL), F32), pltpu.VMEM((tm, D_MODEL), BF16)],
        compiler_params=pltpu.CompilerParams(
            dimension_semantics=("arbitrary",),
            vmem_limit_bytes=VMEM_LIMIT),
        name="post_final" if final else "post",
    )(ym, yr, z, z, x2d, p3d, w_bm, w_br, w_out, g2, w1, b1, w2, b2, g3, wpg, wpe, gf)


def _reorder_in_proj(w_in, b_in):
    big = [(0, 4096), (4104, 4104 + 6144 + 2048)]
    w_z = jnp.concatenate([w_in[..., a:b] for a, b in big], axis=-1)
    b_z = jnp.concatenate([b_in[..., a:b] for a, b in big], axis=-1)
    w_gate = w_in[..., 4096:4104]
    b_gate = b_in[..., 4096:4104]
    return w_z, b_z, w_gate, b_gate


def kernel(x, p, norm1_g, w_in, b_in, conv_w, conv_b, m_norm_g, r_norm_g, w_bm, w_br, w_out,
           norm2_g, w_ff1, b_ff1, w_ff2, b_ff2, norm3_g, w_pe_gate, w_pe, final_g):
    B, S, _ = x.shape
    depth = w_in.shape[0]
    T = B * S
    assert S % TS_MIX == 0 and TS_MIX % CHUNK_L == 0 and S % TM_IN == 0 and T % TM_POST == 0

    w_z, b_z, w_gate, b_gate = _reorder_in_proj(w_in, b_in)
    w_z = w_z.astype(BF16)
    b_z = b_z[:, None, :]
    pad = GATE_LANES - GATE_ROWS
    w_g = jnp.pad(w_gate, ((0, 0), (0, 0), (0, pad))).astype(BF16)
    b_g = jnp.pad(b_gate, ((0, 0), (0, pad)))[:, None, :]

    pos = jnp.arange(S, dtype=F32)
    inv_freq = ROPE_BASE ** (-jnp.arange(0, R_DK, 2, dtype=F32) / R_DK)
    ang = pos[:, None] * inv_freq[None, :]
    cos = jnp.cos(ang)
    sin = jnp.sin(ang)

    bf = lambda w: w.astype(BF16)
    row = lambda v: v[:, None, :]
    w_bm, w_br, w_out, w_ff1, w_ff2, w_pe_gate, w_pe = map(
        bf, (w_bm, w_br, w_out, w_ff1, w_ff2, w_pe_gate, w_pe))
    norm1_g, conv_b, m_norm_g, r_norm_g, norm2_g, b_ff1, b_ff2, norm3_g = map(
        row, (norm1_g, conv_b, m_norm_g, r_norm_g, norm2_g, b_ff1, b_ff2, norm3_g))
    final_g = final_g[None, :]

    x2d = x.reshape(T, D_MODEL)
    p3d = p.reshape(depth, T, PE_DIM)
    for i in range(depth):
        z, gcol, grow = _inproj(i, S, x2d, norm1_g, w_z, b_z, w_g, b_g,
                                conv_w, conv_b, cos, sin)
        ym, yr = _mixer(i, B, S, z, gcol, grow, m_norm_g, r_norm_g)
        x2d = _post(i, i == depth - 1, ym, yr, z, x2d, p3d, w_bm, w_br, w_out,
                    norm2_g, w_ff1, b_ff1, w_ff2, b_ff2, norm3_g, w_pe_gate, w_pe, final_g)
    return x2d.reshape(B, S, D_MODEL)
```

```python
import functools

import numpy as np
import jax
import jax.numpy as jnp
from jax import lax
from jax.experimental import pallas as pl
from jax.experimental.pallas import tpu as pltpu

F32 = jnp.float32
BF16 = jnp.bfloat16

D_MODEL = 1024
PE_DIM = 256
EPS = 1e-6
HEADS = 4
M_DK = 256
M_DV = 256
R_DK = 256
R_DV = 512
M_QK = HEADS * M_DK
M_V = HEADS * M_DV
R_QK = HEADS * R_DK
R_V = HEADS * R_DV
CONV_K = 4
ROPE_BASE = 10000.0
D_FF = 4 * D_MODEL

Z_MQ, Z_MK, Z_MV, Z_MO = 0, 1024, 2048, 3072
Z_RQ, Z_RK, Z_RV, Z_RG = 4096, 5120, 6144, 8192
Z_GM, Z_GR = 10240, 11264
Z_COLS = 12288
GATE_LANES = 128
GATE_ROWS = 8

LANE = 128
VMEM_LIMIT = 60 * 1024 * 1024

TM_IN = 2048
TN_IN = 1024
SUB_ROWS = (512, 512, 512, 512)
EPI_COLS = 256
CONV_COLS = 512
CHUNK_L = 256
TS_MIX = 512
TM_POST = 512
STRIP = 16


def _sigmoid(x):
    return 0.5 * jnp.tanh(0.5 * x) + 0.5


def _swish(x):
    u = 0.5 * x
    return u * jnp.tanh(u) + u


def _log_sigmoid(x):
    return jnp.minimum(x, 0.0) - jnp.log1p(jnp.exp(-jnp.abs(x)))


def _split3(x):
    hi = x.astype(BF16)
    r1 = x - hi.astype(F32)
    mid = r1.astype(BF16)
    lo = (r1 - mid.astype(F32)).astype(BF16)
    return hi, mid, lo


def _dot01_left(mat01, x):
    return sum(jnp.dot(mat01, piece, preferred_element_type=F32) for piece in _split3(x))


def _dot01_right(x, mat01):
    return sum(jnp.dot(piece, mat01, preferred_element_type=F32) for piece in _split3(x))


def _strips(n_rows, body, rows_per_strip=STRIP, unroll=1):
    def step(i, carry):
        body(pl.ds(pl.multiple_of(i * rows_per_strip, rows_per_strip), rows_per_strip))
        return carry
    lax.fori_loop(0, n_rows // rows_per_strip, step, 0, unroll=unroll)


def _inproj_kernel(x_ref, g_ref, w_ref, b_ref, wg_ref, bg_ref,
                   cw_ref, cb_ref, cos_ref, sin_ref,
                   z_ref, gc_ref, gr_ref, h_scr, carry_scr, *, tiles_per_seq):
    i = pl.program_id(0)
    j = pl.program_id(1)

    @pl.when(j == 0)
    def _():
        g = g_ref[...]

        def norm(rows):
            x = x_ref[rows, :]
            ms = jnp.mean(x * x, axis=-1, keepdims=True)
            h_scr[rows, :] = (x * lax.rsqrt(ms + EPS) * g).astype(BF16)
        _strips(TM_IN, norm, unroll=8)
        h = h_scr[...]
        gates = jnp.dot(h, wg_ref[...], preferred_element_type=F32) + bg_ref[...]
        gc_ref[...] = gates
        gr_ref[...] = gates.T[:GATE_ROWS, :]

    def run(epilogue, epi_cols=EPI_COLS, add_bias=True):
        assert sum(SUB_ROWS) == TM_IN
        for r0, n in zip(np.cumsum((0,) + SUB_ROWS[:-1]), SUB_ROWS):
            rows = pl.ds(int(r0), n)
            zb = jnp.dot(h_scr[rows, :], w_ref[...], preferred_element_type=F32)
            if add_bias:
                zb = zb + b_ref[...]
            for c0 in range(0, TN_IN, epi_cols):
                cols = slice(c0, c0 + epi_cols)
                z_ref[rows, cols] = epilogue(zb[:, cols], rows, cols).astype(BF16)

    is_conv = j < Z_MV // TN_IN
    is_rope = (j == Z_RQ // TN_IN) | (j == Z_RK // TN_IN)
    is_sigmoid = (j == Z_MO // TN_IN) | (j >= Z_GM // TN_IN)
    is_swish = (j >= Z_RG // TN_IN) & (j < Z_GM // TN_IN)
    is_plain = jnp.logical_not(is_conv | is_rope | is_sigmoid | is_swish)

    @pl.when(is_plain)
    def _():
        z = jnp.dot(h_scr[...], w_ref[...], preferred_element_type=F32) + b_ref[...]
        z_ref[...] = z.astype(BF16)

    @pl.when(is_sigmoid)
    def _():
        run(lambda zc, rows, cols: _sigmoid(zc))

    @pl.when(is_swish)
    def _():
        run(lambda zc, rows, cols: _swish(zc))

    @pl.when(is_rope)
    def _():
        scale = jnp.where(j == Z_RK // TN_IN, R_DK ** -0.5, 1.0)

        def rope(zc, rows, cols):
            c = cos_ref[rows, :] * scale
            s = sin_ref[rows, :] * scale
            x1, x2 = zc[:, :LANE], zc[:, LANE:]
            return jnp.concatenate([x1 * c - x2 * s, x1 * s + x2 * c], axis=1)
        run(rope)

    @pl.when(is_conv)
    def _():
        bias = b_ref[...]
        cw_half = cw_ref[...] * 0.5
        shift = jnp.sum(cw_half, axis=0, keepdims=True) * bias + cb_ref[...] * 0.5

        @pl.when(i % tiles_per_seq == 0)
        def _():
            carry_scr[j] = jnp.broadcast_to(-bias, carry_scr.shape[1:])
        scale = jnp.where(j == Z_MQ // TN_IN, M_DK ** -0.5, 1.0)

        row7 = lax.broadcasted_iota(jnp.int32, (8, CONV_COLS), 0) == 7

        def down_one(above, cur):
            return pltpu.roll(jnp.where(row7, above, cur), 1, axis=0)

        for r0, n in zip(np.cumsum((0,) + SUB_ROWS[:-1]), SUB_ROWS):
            r0 = int(r0)
            d = jnp.dot(h_scr[pl.ds(r0, n), :], w_ref[...], preferred_element_type=F32)
            for c0 in range(0, TN_IN, CONV_COLS):
                cols = slice(c0, c0 + CONV_COLS)
                w = [cw_half[tap:tap + 1, cols] for tap in range(CONV_K)]
                x = carry_scr[j, :, cols]
                above = [x * w[0]]
                for tap in (1, 2):
                    above.append(x * w[tap] + pltpu.roll(above[-1], 1, axis=0))
                done = []
                for p in range(0, n, 8):
                    x = d[p:p + 8, cols]
                    part = [x * w[0]]
                    for tap in (1, 2, 3):
                        part.append(x * w[tap] + down_one(above[tap - 1], part[-1]))
                    above = part[:3]
                    u = part[3] + shift[:, cols]
                    done.append((u * jnp.tanh(u) + u) * scale)
                    if len(done) == 2:
                        z_ref[pl.ds(r0 + p - 8, 16), cols] = jnp.concatenate(done, axis=0).astype(BF16)
                        done = []
                carry_scr[j, :, cols] = x


def _inproj(layer, seq, x2d, norm_g, w_z, b_z, w_g, b_g, conv_w, conv_b, cos, sin):
    T = x2d.shape[0]
    tiles_per_seq = seq // TM_IN
    n_conv = Z_MV // TN_IN
    grid = (T // TM_IN, Z_COLS // TN_IN)
    return pl.pallas_call(
        functools.partial(_inproj_kernel, tiles_per_seq=tiles_per_seq),
        grid=grid,
        in_specs=[
            pl.BlockSpec((TM_IN, D_MODEL), lambda i, j: (i, 0)),
            pl.BlockSpec((None, 1, D_MODEL), lambda i, j: (layer, 0, 0)),
            pl.BlockSpec((None, D_MODEL, TN_IN), lambda i, j: (layer, 0, j)),
            pl.BlockSpec((None, 1, TN_IN), lambda i, j: (layer, 0, j)),
            pl.BlockSpec((None, D_MODEL, GATE_LANES), lambda i, j: (layer, 0, 0)),
            pl.BlockSpec((None, 1, GATE_LANES), lambda i, j: (layer, 0, 0)),
            pl.BlockSpec((None, CONV_K, TN_IN), lambda i, j: (layer, 0, jnp.minimum(j, n_conv - 1))),
            pl.BlockSpec((None, 1, TN_IN), lambda i, j: (layer, 0, jnp.minimum(j, n_conv - 1))),
            pl.BlockSpec((TM_IN, LANE), lambda i, j: (i % tiles_per_seq, 0)),
            pl.BlockSpec((TM_IN, LANE), lambda i, j: (i % tiles_per_seq, 0)),
        ],
        out_specs=[
            pl.BlockSpec((TM_IN, TN_IN), lambda i, j: (i, j)),
            pl.BlockSpec((TM_IN, GATE_LANES), lambda i, j: (i, 0)),
            pl.BlockSpec((GATE_ROWS, TM_IN), lambda i, j: (0, i)),
        ],
        out_shape=[
            jax.ShapeDtypeStruct((T, Z_COLS), BF16),
            jax.ShapeDtypeStruct((T, GATE_LANES), F32),
            jax.ShapeDtypeStruct((GATE_ROWS, T), F32),
        ],
        scratch_shapes=[
            pltpu.VMEM((TM_IN, D_MODEL), BF16),
            pltpu.VMEM((n_conv, 8, TN_IN), F32),
        ],
        compiler_params=pltpu.CompilerParams(
            dimension_semantics=("arbitrary", "arbitrary"),
            vmem_limit_bytes=VMEM_LIMIT),
        name="inproj",
    )(x2d, norm_g, w_z, b_z, w_g, b_g, conv_w, conv_b, cos, sin)


def _mixer_kernel(q_ref, k_ref, v_ref, o_ref, rq_ref, rk_ref, rv_ref, rg_ref,
                  gc_ref, gr_ref, intra_ref, rcol_ref, mg_ref, rgain_ref,
                  ym_ref, yr_ref,
                  hm_scr, hr_scr, c_scr, r_scr, m_scr, *, cdec):
    L = CHUNK_L
    TS = TS_MIX
    t = pl.program_id(1)

    @pl.when(t == 0)
    def _():
        c_scr[...] = jnp.zeros(c_scr.shape, F32)
        r_scr[...] = jnp.zeros(r_scr.shape, F32)
        m_scr[...] = jnp.zeros(m_scr.shape, F32)

    row_id = lax.broadcasted_iota(jnp.int32, (L, L), 0)
    col_id = lax.broadcasted_iota(jnp.int32, (L, L), 1)
    causal = row_id >= col_id
    tril = causal.astype(BF16)
    triu = (row_id <= col_id).astype(BF16)
    ones_pad = jnp.ones((L, LANE), BF16)
    rcol = rcol_ref[...]

    for c in range(TS // L):
        rows = slice(c * L, (c + 1) * L)
        gcol = gc_ref[rows, :]
        grow = gr_ref[:, rows]
        b_cols = _dot01_left(tril, _log_sigmoid(gcol))
        b_rows = _dot01_right(_log_sigmoid(grow), triu)
        for h in range(HEADS):
            sl = slice(h * M_DK, (h + 1) * M_DK)
            q = q_ref[rows, sl]
            k = k_ref[rows, sl]
            vaug = jnp.concatenate([v_ref[rows, sl], ones_pad], axis=1)
            bcol = b_cols[:, HEADS + h:HEADS + h + 1]
            licol = gcol[:, h:h + 1]
            brow = b_rows[HEADS + h:HEADS + h + 1, :]
            lirow = grow[h:h + 1, :]
            m_prev = m_scr[h][0:1, 0:1]

            log_intra = jnp.where(causal, bcol - brow + lirow, -jnp.inf)
            inter = bcol + m_prev
            m_t = jnp.maximum(inter, jnp.max(log_intra, axis=-1, keepdims=True))
            dmat = jnp.exp(log_intra - m_t)
            d_inter = jnp.exp(inter - m_t)
            s = lax.dot_general(q, k, (((1,), (1,)), ((), ())), preferred_element_type=F32) * dmat
            c_old = c_scr[h]
            num = (jnp.dot(s.astype(BF16), vaug, preferred_element_type=F32)
                   + d_inter * jnp.dot(q, c_old.astype(BF16), preferred_element_type=F32))
            den = num[:, M_DV:M_DV + 1]
            scale = 1.0 / jnp.maximum(jnp.abs(den), jnp.exp(-m_t))
            hm_scr[rows, sl] = num[:, :M_DV] * scale

            b_end = bcol[L - 1:L, :]
            log_w = b_end - bcol + licol
            m_new = jnp.maximum(b_end + m_prev, jnp.max(log_w, axis=0, keepdims=True))
            w = jnp.exp(log_w - m_new)
            decay = jnp.exp(b_end + m_prev - m_new)
            kw = (k.astype(F32) * w).astype(BF16)
            c_scr[h] = decay * c_old + lax.dot_general(kw, vaug, (((0,), (0,)), ((), ())),
                                                       preferred_element_type=F32)
            m_scr[h] = jnp.broadcast_to(m_new, m_scr.shape[1:])

            sl = slice(h * R_DK, (h + 1) * R_DK)
            vs = slice(h * R_DV, (h + 1) * R_DV)
            rq = rq_ref[rows, sl]
            rk = rk_ref[rows, sl]
            rv = rv_ref[rows, vs]
            s = lax.dot_general(rq, rk, (((1,), (1,)), ((), ())), preferred_element_type=F32) * intra_ref[h]
            r_old = r_scr[h]
            hr_scr[rows, vs] = (jnp.dot(s.astype(BF16), rv, preferred_element_type=F32)
                                + jnp.dot(rq, r_old.astype(BF16), preferred_element_type=F32)
                                * rcol[:, h:h + 1])
            rkw = (rk.astype(F32) * rcol[:, HEADS + h:HEADS + h + 1]).astype(BF16)
            r_scr[h] = cdec[h] * r_old + lax.dot_general(rkw, rv, (((0,), (0,)), ((), ())),
                                                         preferred_element_type=F32)

    mg = mg_ref[...]
    rgain = rgain_ref[...]

    def head_norm(x):
        mu = jnp.mean(x, axis=-1, keepdims=True)
        xc = x - mu
        var = jnp.mean(xc * xc, axis=-1, keepdims=True)
        return xc * lax.rsqrt(var + EPS)

    def finish(rows):
        for h in range(HEADS):
            sl = slice(h * M_DV, (h + 1) * M_DV)
            y = o_ref[rows, sl].astype(F32) * (head_norm(hm_scr[rows, sl]) * mg[:, sl])
            ym_ref[rows, sl] = y.astype(BF16)
            vs = slice(h * R_DV, (h + 1) * R_DV)
            y = rg_ref[rows, vs].astype(F32) * (head_norm(hr_scr[rows, vs]) * rgain[:, vs])
            yr_ref[rows, vs] = y.astype(BF16)
    _strips(TS, finish, unroll=4)


def _retention_tables(L):
    gamma = 1.0 - np.exp2(-5.0 - np.arange(HEADS, dtype=np.float64))
    lg = np.log(gamma)
    pos = np.arange(L, dtype=np.float64)
    diff = pos[:, None] - pos[None, :]
    intra = np.where(diff >= 0, np.exp(np.maximum(diff, 0.0) * lg[:, None, None]), 0.0)
    rcol = np.zeros((L, LANE), np.float64)
    rcol[:, :HEADS] = np.exp((pos[:, None] + 1.0) * lg[None, :])
    rcol[:, HEADS:2 * HEADS] = np.exp((L - 1.0 - pos[:, None]) * lg[None, :])
    cdec = tuple(float(v) for v in np.exp(L * lg))
    return jnp.asarray(intra, F32), jnp.asarray(rcol, F32), cdec


def _mixer(layer, batch, seq, z, gcol, grow, m_gain, r_gain):
    L = CHUNK_L
    TS = TS_MIX
    nt = seq // TS
    intra, rcol, cdec = _retention_tables(L)

    def zspec(width, col0):
        blk = col0 // width
        return pl.BlockSpec((TS, width), lambda b, t: (b * nt + t, blk))

    def const(shape):
        return pl.BlockSpec(shape, lambda b, t: tuple(0 for _ in shape))

    def per_layer(width):
        return pl.BlockSpec((None, 1, width), lambda b, t: (layer, 0, 0))

    T = batch * seq
    return pl.pallas_call(
        functools.partial(_mixer_kernel, cdec=cdec),
        grid=(batch, nt),
        in_specs=[
            zspec(M_QK, Z_MQ), zspec(M_QK, Z_MK), zspec(M_V, Z_MV), zspec(M_V, Z_MO),
            zspec(R_QK, Z_RQ), zspec(R_QK, Z_RK), zspec(R_V, Z_RV), zspec(R_V, Z_RG),
            pl.BlockSpec((TS, GATE_LANES), lambda b, t: (b * nt + t, 0)),
            pl.BlockSpec((GATE_ROWS, TS), lambda b, t: (0, b * nt + t)),
            const((HEADS, L, L)), const((L, LANE)),
            per_layer(M_V), per_layer(R_V),
        ],
        out_specs=[
            pl.BlockSpec((TS, M_V), lambda b, t: (b * nt + t, 0)),
            pl.BlockSpec((TS, R_V), lambda b, t: (b * nt + t, 0)),
        ],
        out_shape=[
            jax.ShapeDtypeStruct((T, M_V), BF16),
            jax.ShapeDtypeStruct((T, R_V), BF16),
        ],
        scratch_shapes=[
            pltpu.VMEM((TS, M_V), F32), pltpu.VMEM((TS, R_V), F32),
            pltpu.VMEM((HEADS, M_DK, M_DV + LANE), F32),
            pltpu.VMEM((HEADS, R_DK, R_DV), F32),
            pltpu.VMEM((HEADS, 8, LANE), F32),
        ],
        compiler_params=pltpu.CompilerParams(
            dimension_semantics=("arbitrary", "arbitrary"),
            vmem_limit_bytes=VMEM_LIMIT),
        name="mixer",
    )(z, z, z, z, z, z, z, z, gcol, grow, intra, rcol, m_gain, r_gain)


def _rmsnorm_rows(src_ref, dst_ref, g, n_rows):
    def norm(rows):
        x = src_ref[rows, :]
        ms = jnp.mean(x * x, axis=-1, keepdims=True)
        dst_ref[rows, :] = (x * lax.rsqrt(ms + EPS) * g).astype(dst_ref.dtype)
    _strips(n_rows, norm, unroll=8)


def _post_kernel(ym_ref, yr_ref, gm_ref, gr_ref, x_ref, p_ref,
                 wbm_ref, wbr_ref, wout_ref, g2_ref, w1_ref, b1_ref, w2_ref, b2_ref,
                 g3_ref, wpg_ref, wpe_ref, gf_ref,
                 out_ref, x_scr, h_scr, *, final):
    a = jnp.dot(ym_ref[...], wbm_ref[...], preferred_element_type=F32)
    merged = gm_ref[...].astype(F32) * a
    b = jnp.dot(yr_ref[...], wbr_ref[...], preferred_element_type=F32)
    merged = merged + gr_ref[...].astype(F32) * b
    x_scr[...] = x_ref[...] + jnp.dot(merged.astype(BF16), wout_ref[...], preferred_element_type=F32)

    _rmsnorm_rows(x_scr, h_scr, g2_ref[...], TM_POST)
    h = h_scr[...]
    acc = x_scr[...] + b2_ref[...]
    for c in range(D_FF // D_MODEL):
        cs = slice(c * D_MODEL, (c + 1) * D_MODEL)
        u = jnp.maximum(jnp.dot(h, w1_ref[:, cs], preferred_element_type=F32) + b1_ref[:, cs], 0.0)
        acc = acc + jnp.dot((u * u).astype(BF16), w2_ref[cs, :], preferred_element_type=F32)
    x_scr[...] = acc

    _rmsnorm_rows(x_scr, h_scr, g3_ref[...], TM_POST)
    gate = _sigmoid(jnp.dot(h_scr[...], wpg_ref[...], preferred_element_type=F32))
    pe = jnp.dot(p_ref[...].astype(BF16), wpe_ref[...], preferred_element_type=F32)
    if final:
        x_scr[...] = x_scr[...] + gate * pe
        _rmsnorm_rows(x_scr, out_ref, gf_ref[...], TM_POST)
    else:
        out_ref[...] = x_scr[...] + gate * pe


def _post(layer, final, ym, yr, z, x2d, p3d, w_bm, w_br, w_out, g2, w1, b1, w2, b2, g3, wpg, wpe, gf):
    T = x2d.shape[0]
    tm = TM_POST

    def rows(width, blk=0):
        return pl.BlockSpec((tm, width), lambda i: (i, blk))

    def weight(k, n):
        return pl.BlockSpec((None, k, n), lambda i: (layer, 0, 0), pipeline_mode=pl.Buffered(1))

    def vec(n):
        return pl.BlockSpec((None, 1, n), lambda i: (layer, 0, 0))

    return pl.pallas_call(
        functools.partial(_post_kernel, final=final),
        grid=(T // tm,),
        in_specs=[
            rows(M_V), rows(R_V), rows(D_MODEL, Z_GM // D_MODEL), rows(D_MODEL, Z_GR // D_MODEL),
            rows(D_MODEL),
            pl.BlockSpec((None, tm, PE_DIM), lambda i: (layer, i, 0)),
            weight(M_V, D_MODEL), weight(R_V, D_MODEL), weight(D_MODEL, D_MODEL),
            vec(D_MODEL), weight(D_MODEL, D_FF), vec(D_FF), weight(D_FF, D_MODEL), vec(D_MODEL),
            vec(D_MODEL), weight(D_MODEL, D_MODEL), weight(PE_DIM, D_MODEL),
            pl.BlockSpec((1, D_MODEL), lambda i: (0, 0)),
        ],
        out_specs=rows(D_MODEL),
        out_shape=jax.ShapeDtypeStruct((T, D_MODEL), F32),
        scratch_shapes=[pltpu.VMEM((tm, D_MODEL), F32), pltpu.VMEM((tm, D_MODEL), BF16)],
        compiler_params=pltpu.CompilerParams(
            dimension_semantics=("arbitrary",),
            vmem_limit_bytes=VMEM_LIMIT),
        name="post_final" if final else "post",
    )(ym, yr, z, z, x2d, p3d, w_bm, w_br, w_out, g2, w1, b1, w2, b2, g3, wpg, wpe, gf)


def _reorder_in_proj(w_in, b_in):
    big = [(0, 4096), (4104, 4104 + 6144 + 2048)]
    w_z = jnp.concatenate([w_in[..., a:b] for a, b in big], axis=-1)
    b_z = jnp.concatenate([b_in[..., a:b] for a, b in big], axis=-1)
    w_gate = w_in[..., 4096:4104]
    b_gate = b_in[..., 4096:4104]
    return w_z, b_z, w_gate, b_gate


def kernel(x, p, norm1_g, w_in, b_in, conv_w, conv_b, m_norm_g, r_norm_g, w_bm, w_br, w_out,
           norm2_g, w_ff1, b_ff1, w_ff2, b_ff2, norm3_g, w_pe_gate, w_pe, final_g):
    B, S, _ = x.shape
    depth = w_in.shape[0]
    T = B * S
    assert S % TS_MIX == 0 and TS_MIX % CHUNK_L == 0 and S % TM_IN == 0 and T % TM_POST == 0

    w_z, b_z, w_gate, b_gate = _reorder_in_proj(w_in, b_in)
    w_z = w_z.astype(BF16)
    b_z = b_z[:, None, :]
    pad = GATE_LANES - GATE_ROWS
    w_g = jnp.pad(w_gate, ((0, 0), (0, 0), (0, pad))).astype(BF16)
    b_g = jnp.pad(b_gate, ((0, 0), (0, pad)))[:, None, :]

    pos = jnp.arange(S, dtype=F32)
    inv_freq = ROPE_BASE ** (-jnp.arange(0, R_DK, 2, dtype=F32) / R_DK)
    ang = pos[:, None] * inv_freq[None, :]
    cos = jnp.cos(ang)
    sin = jnp.sin(ang)

    bf = lambda w: w.astype(BF16)
    row = lambda v: v[:, None, :]
    w_bm, w_br, w_out, w_ff1, w_ff2, w_pe_gate, w_pe = map(
        bf, (w_bm, w_br, w_out, w_ff1, w_ff2, w_pe_gate, w_pe))
    norm1_g, conv_b, m_norm_g, r_norm_g, norm2_g, b_ff1, b_ff2, norm3_g = map(
        row, (norm1_g, conv_b, m_norm_g, r_norm_g, norm2_g, b_ff1, b_ff2, norm3_g))
    final_g = final_g[None, :]

    x2d = x.reshape(T, D_MODEL)
    p3d = p.reshape(depth, T, PE_DIM)
    for i in range(depth):
        z, gcol, grow = _inproj(i, S, x2d, norm1_g, w_z, b_z, w_g, b_g,
                                conv_w, conv_b, cos, sin)
        ym, yr = _mixer(i, B, S, z, gcol, grow, m_norm_g, r_norm_g)
        x2d = _post(i, i == depth - 1, ym, yr, z, x2d, p3d, w_bm, w_br, w_out,
                    norm2_g, w_ff1, b_ff1, w_ff2, b_ff2, norm3_g, w_pe_gate, w_pe, final_g)
    return x2d.reshape(B, S, D_MODEL)
```

```python
import functools

import numpy as np
import jax
import jax.numpy as jnp
from jax import lax
from jax.experimental import pallas as pl
from jax.experimental.pallas import tpu as pltpu

F32 = jnp.float32
BF16 = jnp.bfloat16

D_MODEL = 1024
PE_DIM = 256
EPS = 1e-6
HEADS = 4
M_DK = 256
M_DV = 256
R_DK = 256
R_DV = 512
M_QK = HEADS * M_DK
M_V = HEADS * M_DV
R_QK = HEADS * R_DK
R_V = HEADS * R_DV
CONV_K = 4
ROPE_BASE = 10000.0
D_FF = 4 * D_MODEL

Z_MQ, Z_MK, Z_MV, Z_MO = 0, 1024, 2048, 3072
Z_RQ, Z_RK, Z_RV, Z_RG = 4096, 5120, 6144, 8192
Z_GM, Z_GR = 10240, 11264
Z_COLS = 12288
GATE_LANES = 128
GATE_ROWS = 8

LANE = 128
VMEM_LIMIT = 60 * 1024 * 1024

TM_IN = 2048
TN_IN = 1024
SUB_ROWS = (512, 512, 512, 512)
EPI_COLS = 256
CONV_COLS = 512
CHUNK_L = 256
TS_MIX = 512
TM_POST = 512
STRIP = 16


def _sigmoid(x):
    return 0.5 * jnp.tanh(0.5 * x) + 0.5


def _swish(x):
    u = 0.5 * x
    return u * jnp.tanh(u) + u


def _log_sigmoid(x):
    return jnp.minimum(x, 0.0) - jnp.log1p(jnp.exp(-jnp.abs(x)))


def _split3(x):
    hi = x.astype(BF16)
    r1 = x - hi.astype(F32)
    mid = r1.astype(BF16)
    lo = (r1 - mid.astype(F32)).astype(BF16)
    return hi, mid, lo


def _dot01_left(mat01, x):
    return sum(jnp.dot(mat01, piece, preferred_element_type=F32) for piece in _split3(x))


def _dot01_right(x, mat01):
    return sum(jnp.dot(piece, mat01, preferred_element_type=F32) for piece in _split3(x))


def _strips(n_rows, body, rows_per_strip=STRIP, unroll=1):
    def step(i, carry):
        body(pl.ds(pl.multiple_of(i * rows_per_strip, rows_per_strip), rows_per_strip))
        return carry
    lax.fori_loop(0, n_rows // rows_per_strip, step, 0, unroll=unroll)


def _inproj_kernel(x_ref, g_ref, w_ref, b_ref, wg_ref, bg_ref,
                   cw_ref, cb_ref, cos_ref, sin_ref,
                   z_ref, gc_ref, gr_ref, h_scr, carry_scr, *, tiles_per_seq):
    i = pl.program_id(0)
    j = pl.program_id(1)

    @pl.when(j == 0)
    def _():
        g = g_ref[...]

        def norm(rows):
            x = x_ref[rows, :]
            ms = jnp.mean(x * x, axis=-1, keepdims=True)
            h_scr[rows, :] = (x * lax.rsqrt(ms + EPS) * g).astype(BF16)
        _strips(TM_IN, norm, unroll=16)
        h = h_scr[...]
        gates = jnp.dot(h, wg_ref[...], preferred_element_type=F32) + bg_ref[...]
        gc_ref[...] = gates
        gr_ref[...] = gates.T[:GATE_ROWS, :]

    def run(epilogue, epi_cols=EPI_COLS, add_bias=True):
        assert sum(SUB_ROWS) == TM_IN
        for r0, n in zip(np.cumsum((0,) + SUB_ROWS[:-1]), SUB_ROWS):
            rows = pl.ds(int(r0), n)
            zb = jnp.dot(h_scr[rows, :], w_ref[...], preferred_element_type=F32)
            if add_bias:
                zb = zb + b_ref[...]
            for c0 in range(0, TN_IN, epi_cols):
                cols = slice(c0, c0 + epi_cols)
                z_ref[rows, cols] = epilogue(zb[:, cols], rows, cols).astype(BF16)

    is_conv = j < Z_MV // TN_IN
    is_rope = (j == Z_RQ // TN_IN) | (j == Z_RK // TN_IN)
    is_sigmoid = (j == Z_MO // TN_IN) | (j >= Z_GM // TN_IN)
    is_swish = (j >= Z_RG // TN_IN) & (j < Z_GM // TN_IN)
    is_plain = jnp.logical_not(is_conv | is_rope | is_sigmoid | is_swish)

    @pl.when(is_plain)
    def _():
        z = jnp.dot(h_scr[...], w_ref[...], preferred_element_type=F32) + b_ref[...]
        z_ref[...] = z.astype(BF16)

    @pl.when(is_sigmoid)
    def _():
        run(lambda zc, rows, cols: _sigmoid(zc))

    @pl.when(is_swish)
    def _():
        run(lambda zc, rows, cols: _swish(zc))

    @pl.when(is_rope)
    def _():
        scale = jnp.where(j == Z_RK // TN_IN, R_DK ** -0.5, 1.0)

        def rope(zc, rows, cols):
            c = cos_ref[rows, :] * scale
            s = sin_ref[rows, :] * scale
            x1, x2 = zc[:, :LANE], zc[:, LANE:]
            return jnp.concatenate([x1 * c - x2 * s, x1 * s + x2 * c], axis=1)
        run(rope)

    @pl.when(is_conv)
    def _():
        bias = b_ref[...]
        cw_half = cw_ref[...] * 0.5
        shift = jnp.sum(cw_half, axis=0, keepdims=True) * bias + cb_ref[...] * 0.5

        @pl.when(i % tiles_per_seq == 0)
        def _():
            carry_scr[j] = jnp.broadcast_to(-bias, carry_scr.shape[1:])
        scale = jnp.where(j == Z_MQ // TN_IN, M_DK ** -0.5, 1.0)

        row7 = lax.broadcasted_iota(jnp.int32, (8, CONV_COLS), 0) == 7

        def down_one(above, cur):
            return pltpu.roll(jnp.where(row7, above, cur), 1, axis=0)

        for r0, n in zip(np.cumsum((0,) + SUB_ROWS[:-1]), SUB_ROWS):
            r0 = int(r0)
            d = jnp.dot(h_scr[pl.ds(r0, n), :], w_ref[...], preferred_element_type=F32)
            for c0 in range(0, TN_IN, CONV_COLS):
                cols = slice(c0, c0 + CONV_COLS)
                w = [cw_half[tap:tap + 1, cols] for tap in range(CONV_K)]
                x = carry_scr[j, :, cols]
                above = [x * w[0]]
                for tap in (1, 2):
                    above.append(x * w[tap] + pltpu.roll(above[-1], 1, axis=0))
                done = []
                for p in range(0, n, 8):
                    x = d[p:p + 8, cols]
                    part = [x * w[0]]
                    for tap in (1, 2, 3):
                        part.append(x * w[tap] + down_one(above[tap - 1], part[-1]))
                    above = part[:3]
                    u = part[3] + shift[:, cols]
                    done.append((u * jnp.tanh(u) + u) * scale)
                    if len(done) == 2:
                        z_ref[pl.ds(r0 + p - 8, 16), cols] = jnp.concatenate(done, axis=0).astype(BF16)
                        done = []
                carry_scr[j, :, cols] = x


def _inproj(layer, seq, x2d, norm_g, w_z, b_z, w_g, b_g, conv_w, conv_b, cos, sin):
    T = x2d.shape[0]
    tiles_per_seq = seq // TM_IN
    n_conv = Z_MV // TN_IN
    grid = (T // TM_IN, Z_COLS // TN_IN)
    return pl.pallas_call(
        functools.partial(_inproj_kernel, tiles_per_seq=tiles_per_seq),
        grid=grid,
        in_specs=[
            pl.BlockSpec((TM_IN, D_MODEL), lambda i, j: (i, 0)),
            pl.BlockSpec((None, 1, D_MODEL), lambda i, j: (layer, 0, 0)),
            pl.BlockSpec((None, D_MODEL, TN_IN), lambda i, j: (layer, 0, j)),
            pl.BlockSpec((None, 1, TN_IN), lambda i, j: (layer, 0, j)),
            pl.BlockSpec((None, D_MODEL, GATE_LANES), lambda i, j: (layer, 0, 0)),
            pl.BlockSpec((None, 1, GATE_LANES), lambda i, j: (layer, 0, 0)),
            pl.BlockSpec((None, CONV_K, TN_IN), lambda i, j: (layer, 0, jnp.minimum(j, n_conv - 1))),
            pl.BlockSpec((None, 1, TN_IN), lambda i, j: (layer, 0, jnp.minimum(j, n_conv - 1))),
            pl.BlockSpec((TM_IN, LANE), lambda i, j: (i % tiles_per_seq, 0)),
            pl.BlockSpec((TM_IN, LANE), lambda i, j: (i % tiles_per_seq, 0)),
        ],
        out_specs=[
            pl.BlockSpec((TM_IN, TN_IN), lambda i, j: (i, j)),
            pl.BlockSpec((TM_IN, GATE_LANES), lambda i, j: (i, 0)),
            pl.BlockSpec((GATE_ROWS, TM_IN), lambda i, j: (0, i)),
        ],
        out_shape=[
            jax.ShapeDtypeStruct((T, Z_COLS), BF16),
            jax.ShapeDtypeStruct((T, GATE_LANES), F32),
            jax.ShapeDtypeStruct((GATE_ROWS, T), F32),
        ],
        scratch_shapes=[
            pltpu.VMEM((TM_IN, D_MODEL), BF16),
            pltpu.VMEM((n_conv, 8, TN_IN), F32),
        ],
        compiler_params=pltpu.CompilerParams(
            dimension_semantics=("arbitrary", "arbitrary"),
            vmem_limit_bytes=VMEM_LIMIT),
        name="inproj",
    )(x2d, norm_g, w_z, b_z, w_g, b_g, conv_w, conv_b, cos, sin)


def _mixer_kernel(q_ref, k_ref, v_ref, o_ref, rq_ref, rk_ref, rv_ref, rg_ref,
                  gc_ref, gr_ref, intra_ref, rcol_ref, mg_ref, rgain_ref,
                  ym_ref, yr_ref,
                  hm_scr, hr_scr, c_scr, r_scr, m_scr, *, cdec):
    L = CHUNK_L
    TS = TS_MIX
    t = pl.program_id(1)

    @pl.when(t == 0)
    def _():
        c_scr[...] = jnp.zeros(c_scr.shape, F32)
        r_scr[...] = jnp.zeros(r_scr.shape, F32)
        m_scr[...] = jnp.zeros(m_scr.shape, F32)

    row_id = lax.broadcasted_iota(jnp.int32, (L, L), 0)
    col_id = lax.broadcasted_iota(jnp.int32, (L, L), 1)
    causal = row_id >= col_id
    tril = causal.astype(BF16)
    triu = (row_id <= col_id).astype(BF16)
    ones_pad = jnp.ones((L, LANE), BF16)
    rcol = rcol_ref[...]

    for c in range(TS // L):
        rows = slice(c * L, (c + 1) * L)
        gcol = gc_ref[rows, :]
        grow = gr_ref[:, rows]
        b_cols = _dot01_left(tril, _log_sigmoid(gcol))
        b_rows = _dot01_right(_log_sigmoid(grow), triu)
        for h in range(HEADS):
            sl = slice(h * M_DK, (h + 1) * M_DK)
            q = q_ref[rows, sl]
            k = k_ref[rows, sl]
            vaug = jnp.concatenate([v_ref[rows, sl], ones_pad], axis=1)
            bcol = b_cols[:, HEADS + h:HEADS + h + 1]
            licol = gcol[:, h:h + 1]
            brow = b_rows[HEADS + h:HEADS + h + 1, :]
            lirow = grow[h:h + 1, :]
            m_prev = m_scr[h][0:1, 0:1]

            log_intra = jnp.where(causal, bcol - brow + lirow, -jnp.inf)
            inter = bcol + m_prev
            m_t = jnp.maximum(inter, jnp.max(log_intra, axis=-1, keepdims=True))
            dmat = jnp.exp(log_intra - m_t)
            d_inter = jnp.exp(inter - m_t)
            s = lax.dot_general(q, k, (((1,), (1,)), ((), ())), preferred_element_type=F32) * dmat
            c_old = c_scr[h]
            num = (jnp.dot(s.astype(BF16), vaug, preferred_element_type=F32)
                   + d_inter * jnp.dot(q, c_old.astype(BF16), preferred_element_type=F32))
            den = num[:, M_DV:M_DV + 1]
            scale = 1.0 / jnp.maximum(jnp.abs(den), jnp.exp(-m_t))
            hm_scr[rows, sl] = num[:, :M_DV] * scale

            b_end = bcol[L - 1:L, :]
            log_w = b_end - bcol + licol
            m_new = jnp.maximum(b_end + m_prev, jnp.max(log_w, axis=0, keepdims=True))
            w = jnp.exp(log_w - m_new)
            decay = jnp.exp(b_end + m_prev - m_new)
            kw = (k.astype(F32) * w).astype(BF16)
            c_scr[h] = decay * c_old + lax.dot_general(kw, vaug, (((0,), (0,)), ((), ())),
                                                       preferred_element_type=F32)
            m_scr[h] = jnp.broadcast_to(m_new, m_scr.shape[1:])

            sl = slice(h * R_DK, (h + 1) * R_DK)
            vs = slice(h * R_DV, (h + 1) * R_DV)
            rq = rq_ref[rows, sl]
            rk = rk_ref[rows, sl]
            rv = rv_ref[rows, vs]
            s = lax.dot_general(rq, rk, (((1,), (1,)), ((), ())), preferred_element_type=F32) * intra_ref[h]
            r_old = r_scr[h]
            hr_scr[rows, vs] = (jnp.dot(s.astype(BF16), rv, preferred_element_type=F32)
                                + jnp.dot(rq, r_old.astype(BF16), preferred_element_type=F32)
                                * rcol[:, h:h + 1])
            rkw = (rk.astype(F32) * rcol[:, HEADS + h:HEADS + h + 1]).astype(BF16)
            r_scr[h] = cdec[h] * r_old + lax.dot_general(rkw, rv, (((0,), (0,)), ((), ())),
                                                         preferred_element_type=F32)

    mg = mg_ref[...]
    rgain = rgain_ref[...]

    def head_norm(x):
        mu = jnp.mean(x, axis=-1, keepdims=True)
        xc = x - mu
        var = jnp.mean(xc * xc, axis=-1, keepdims=True)
        return xc * lax.rsqrt(var + EPS)

    def finish(rows):
        for h in range(HEADS):
            sl = slice(h * M_DV, (h + 1) * M_DV)
            y = o_ref[rows, sl].astype(F32) * (head_norm(hm_scr[rows, sl]) * mg[:, sl])
            ym_ref[rows, sl] = y.astype(BF16)
            vs = slice(h * R_DV, (h + 1) * R_DV)
            y = rg_ref[rows, vs].astype(F32) * (head_norm(hr_scr[rows, vs]) * rgain[:, vs])
            yr_ref[rows, vs] = y.astype(BF16)
    _strips(TS, finish, unroll=4)


def _retention_tables(L):
    gamma = 1.0 - np.exp2(-5.0 - np.arange(HEADS, dtype=np.float64))
    lg = np.log(gamma)
    pos = np.arange(L, dtype=np.float64)
    diff = pos[:, None] - pos[None, :]
    intra = np.where(diff >= 0, np.exp(np.maximum(diff, 0.0) * lg[:, None, None]), 0.0)
    rcol = np.zeros((L, LANE), np.float64)
    rcol[:, :HEADS] = np.exp((pos[:, None] + 1.0) * lg[None, :])
    rcol[:, HEADS:2 * HEADS] = np.exp((L - 1.0 - pos[:, None]) * lg[None, :])
    cdec = tuple(float(v) for v in np.exp(L * lg))
    return jnp.asarray(intra, F32), jnp.asarray(rcol, F32), cdec


def _mixer(layer, batch, seq, z, gcol, grow, m_gain, r_gain):
    L = CHUNK_L
    TS = TS_MIX
    nt = seq // TS
    intra, rcol, cdec = _retention_tables(L)

    def zspec(width, col0):
        blk = col0 // width
        return pl.BlockSpec((TS, width), lambda b, t: (b * nt + t, blk))

    def const(shape):
        return pl.BlockSpec(shape, lambda b, t: tuple(0 for _ in shape))

    def per_layer(width):
        return pl.BlockSpec((None, 1, width), lambda b, t: (layer, 0, 0))

    T = batch * seq
    return pl.pallas_call(
        functools.partial(_mixer_kernel, cdec=cdec),
        grid=(batch, nt),
        in_specs=[
            zspec(M_QK, Z_MQ), zspec(M_QK, Z_MK), zspec(M_V, Z_MV), zspec(M_V, Z_MO),
            zspec(R_QK, Z_RQ), zspec(R_QK, Z_RK), zspec(R_V, Z_RV), zspec(R_V, Z_RG),
            pl.BlockSpec((TS, GATE_LANES), lambda b, t: (b * nt + t, 0)),
            pl.BlockSpec((GATE_ROWS, TS), lambda b, t: (0, b * nt + t)),
            const((HEADS, L, L)), const((L, LANE)),
            per_layer(M_V), per_layer(R_V),
        ],
        out_specs=[
            pl.BlockSpec((TS, M_V), lambda b, t: (b * nt + t, 0)),
            pl.BlockSpec((TS, R_V), lambda b, t: (b * nt + t, 0)),
        ],
        out_shape=[
            jax.ShapeDtypeStruct((T, M_V), BF16),
            jax.ShapeDtypeStruct((T, R_V), BF16),
        ],
        scratch_shapes=[
            pltpu.VMEM((TS, M_V), F32), pltpu.VMEM((TS, R_V), F32),
            pltpu.VMEM((HEADS, M_DK, M_DV + LANE), F32),
            pltpu.VMEM((HEADS, R_DK, R_DV), F32),
            pltpu.VMEM((HEADS, 8, LANE), F32),
        ],
        compiler_params=pltpu.CompilerParams(
            dimension_semantics=("arbitrary", "arbitrary"),
            vmem_limit_bytes=VMEM_LIMIT),
        name="mixer",
    )(z, z, z, z, z, z, z, z, gcol, grow, intra, rcol, m_gain, r_gain)


def _rmsnorm_rows(src_ref, dst_ref, g, n_rows):
    def norm(rows):
        x = src_ref[rows, :]
        ms = jnp.mean(x * x, axis=-1, keepdims=True)
        dst_ref[rows, :] = (x * lax.rsqrt(ms + EPS) * g).astype(dst_ref.dtype)
    _strips(n_rows, norm, unroll=16)


def _post_kernel(ym_ref, yr_ref, gm_ref, gr_ref, x_ref, p_ref,
                 wbm_ref, wbr_ref, wout_ref, g2_ref, w1_ref, b1_ref, w2_ref, b2_ref,
                 g3_ref, wpg_ref, wpe_ref, gf_ref,
                 out_ref, x_scr, h_scr, *, final):
    a = jnp.dot(ym_ref[...], wbm_ref[...], preferred_element_type=F32)
    merged = gm_ref[...].astype(F32) * a
    b = jnp.dot(yr_ref[...], wbr_ref[...], preferred_element_type=F32)
    merged = merged + gr_ref[...].astype(F32) * b
    x_scr[...] = x_ref[...] + jnp.dot(merged.astype(BF16), wout_ref[...], preferred_element_type=F32)

    _rmsnorm_rows(x_scr, h_scr, g2_ref[...], TM_POST)
    h = h_scr[...]
    acc = x_scr[...] + b2_ref[...]
    for c in range(D_FF // D_MODEL):
        cs = slice(c * D_MODEL, (c + 1) * D_MODEL)
        u = jnp.maximum(jnp.dot(h, w1_ref[:, cs], preferred_element_type=F32) + b1_ref[:, cs], 0.0)
        acc = acc + jnp.dot((u * u).astype(BF16), w2_ref[cs, :], preferred_element_type=F32)
    x_scr[...] = acc

    _rmsnorm_rows(x_scr, h_scr, g3_ref[...], TM_POST)
    gate = _sigmoid(jnp.dot(h_scr[...], wpg_ref[...], preferred_element_type=F32))
    pe = jnp.dot(p_ref[...].astype(BF16), wpe_ref[...], preferred_element_type=F32)
    if final:
        x_scr[...] = x_scr[...] + gate * pe
        _rmsnorm_rows(x_scr, out_ref, gf_ref[...], TM_POST)
    else:
        out_ref[...] = x_scr[...] + gate * pe


def _post(layer, final, ym, yr, z, x2d, p3d, w_bm, w_br, w_out, g2, w1, b1, w2, b2, g3, wpg, wpe, gf):
    T = x2d.shape[0]
    tm = TM_POST

    def rows(width, blk=0):
        return pl.BlockSpec((tm, width), lambda i: (i, blk))

    def weight(k, n):
        return pl.BlockSpec((None, k, n), lambda i: (layer, 0, 0), pipeline_mode=pl.Buffered(1))

    def vec(n):
        return pl.BlockSpec((None, 1, n), lambda i: (layer, 0, 0))

    return pl.pallas_call(
        functools.partial(_post_kernel, final=final),
        grid=(T // tm,),
        in_specs=[
            rows(M_V), rows(R_V), rows(D_MODEL, Z_GM // D_MODEL), rows(D_MODEL, Z_GR // D_MODEL),
            rows(D_MODEL),
            pl.BlockSpec((None, tm, PE_DIM), lambda i: (layer, i, 0)),
            weight(M_V, D_MODEL), weight(R_V, D_MODEL), weight(D_MODEL, D_MODEL),
            vec(D_MODEL), weight(D_MODEL, D_FF), vec(D_FF), weight(D_FF, D_MODEL), vec(D_MODEL),
            vec(D_MODEL), weight(D_MODEL, D_MODEL), weight(PE_DIM, D_MODEL),
            pl.BlockSpec((1, D_MODEL), lambda i: (0, 0)),
        ],
        out_specs=rows(D_MODEL),
        out_shape=jax.ShapeDtypeStruct((T, D_MODEL), F32),
        scratch_shapes=[pltpu.VMEM((tm, D_MODEL), F32), pltpu.VMEM((tm, D_MODEL), BF16)],
        compiler_params=pltpu.CompilerParams(
            dimension_semantics=("arbitrary",),
            vmem_limit_bytes=VMEM_LIMIT),
        name="post_final" if final else "post",
    )(ym, yr, z, z, x2d, p3d, w_bm, w_br, w_out, g2, w1, b1, w2, b2, g3, wpg, wpe, gf)


def _reorder_in_proj(w_in, b_in):
    big = [(0, 4096), (4104, 4104 + 6144 + 2048)]
    w_z = jnp.concatenate([w_in[..., a:b].astype(BF16) for a, b in big], axis=-1)
    b_z = jnp.concatenate([b_in[..., a:b] for a, b in big], axis=-1)
    w_gate = w_in[..., 4096:4104]
    b_gate = b_in[..., 4096:4104]
    return w_z, b_z, w_gate, b_gate


def kernel(x, p, norm1_g, w_in, b_in, conv_w, conv_b, m_norm_g, r_norm_g, w_bm, w_br, w_out,
           norm2_g, w_ff1, b_ff1, w_ff2, b_ff2, norm3_g, w_pe_gate, w_pe, final_g):
    B, S, _ = x.shape
    depth = w_in.shape[0]
    T = B * S
    assert S % TS_MIX == 0 and TS_MIX % CHUNK_L == 0 and S % TM_IN == 0 and T % TM_POST == 0

    w_z, b_z, w_gate, b_gate = _reorder_in_proj(w_in, b_in)
    b_z = b_z[:, None, :]
    pad = GATE_LANES - GATE_ROWS
    w_g = jnp.pad(w_gate, ((0, 0), (0, 0), (0, pad))).astype(BF16)
    b_g = jnp.pad(b_gate, ((0, 0), (0, pad)))[:, None, :]

    pos = jnp.arange(S, dtype=F32)
    inv_freq = ROPE_BASE ** (-jnp.arange(0, R_DK, 2, dtype=F32) / R_DK)
    ang = pos[:, None] * inv_freq[None, :]
    cos = jnp.cos(ang)
    sin = jnp.sin(ang)

    bf = lambda w: w.astype(BF16)
    row = lambda v: v[:, None, :]
    w_bm, w_br, w_out, w_ff1, w_ff2, w_pe_gate, w_pe = map(
        bf, (w_bm, w_br, w_out, w_ff1, w_ff2, w_pe_gate, w_pe))
    norm1_g, conv_b, m_norm_g, r_norm_g, norm2_g, b_ff1, b_ff2, norm3_g = map(
        row, (norm1_g, conv_b, m_norm_g, r_norm_g, norm2_g, b_ff1, b_ff2, norm3_g))
    final_g = final_g[None, :]

    x2d = x.reshape(T, D_MODEL)
    p3d = p.reshape(depth, T, PE_DIM)
    for i in range(depth):
        z, gcol, grow = _inproj(i, S, x2d, norm1_g, w_z, b_z, w_g, b_g,
                                conv_w, conv_b, cos, sin)
        ym, yr = _mixer(i, B, S, z, gcol, grow, m_norm_g, r_norm_g)
        x2d = _post(i, i == depth - 1, ym, yr, z, x2d, p3d, w_bm, w_br, w_out,
                    norm2_g, w_ff1, b_ff1, w_ff2, b_ff2, norm3_g, w_pe_gate, w_pe, final_g)
    return x2d.reshape(B, S, D_MODEL)
```

```python
import functools

import numpy as np
import jax
import jax.numpy as jnp
from jax import lax
from jax.experimental import pallas as pl
from jax.experimental.pallas import tpu as pltpu

F32 = jnp.float32
BF16 = jnp.bfloat16

D_MODEL = 1024
PE_DIM = 256
EPS = 1e-6
HEADS = 4
M_DK = 256
M_DV = 256
R_DK = 256
R_DV = 512
M_QK = HEADS * M_DK
M_V = HEADS * M_DV
R_QK = HEADS * R_DK
R_V = HEADS * R_DV
CONV_K = 4
ROPE_BASE = 10000.0
D_FF = 4 * D_MODEL

Z_MQ, Z_MK, Z_MV, Z_MO = 0, 1024, 2048, 3072
Z_RQ, Z_RK, Z_RV, Z_RG = 4096, 5120, 6144, 8192
Z_GM, Z_GR = 10240, 11264
Z_COLS = 12288
GATE_LANES = 128
GATE_ROWS = 8

LANE = 128
VMEM_LIMIT = 60 * 1024 * 1024

TM_IN = 2048
TN_IN = 1024
SUB_ROWS = (512, 512, 512, 512)
EPI_COLS = 256
CONV_COLS = 512
CHUNK_L = 256
TS_MIX = 256
TM_POST = 512
STRIP = 16


def _sigmoid(x):
    return 0.5 * jnp.tanh(0.5 * x) + 0.5


def _swish(x):
    u = 0.5 * x
    return u * jnp.tanh(u) + u


def _log_sigmoid(x):
    return jnp.minimum(x, 0.0) - jnp.log1p(jnp.exp(-jnp.abs(x)))


def _split3(x):
    hi = x.astype(BF16)
    r1 = x - hi.astype(F32)
    mid = r1.astype(BF16)
    lo = (r1 - mid.astype(F32)).astype(BF16)
    return hi, mid, lo


def _dot01_left(mat01, x):
    return sum(jnp.dot(mat01, piece, preferred_element_type=F32) for piece in _split3(x))


def _dot01_right(x, mat01):
    return sum(jnp.dot(piece, mat01, preferred_element_type=F32) for piece in _split3(x))


def _strips(n_rows, body, rows_per_strip=STRIP, unroll=1):
    def step(i, carry):
        body(pl.ds(pl.multiple_of(i * rows_per_strip, rows_per_strip), rows_per_strip))
        return carry
    lax.fori_loop(0, n_rows // rows_per_strip, step, 0, unroll=unroll)


def _inproj_kernel(x_ref, g_ref, w_ref, b_ref, wg_ref, bg_ref,
                   cw_ref, cb_ref, cos_ref, sin_ref,
                   z_ref, gc_ref, gr_ref, h_scr, carry_scr, *, tiles_per_seq):
    i = pl.program_id(0)
    j = pl.program_id(1)

    @pl.when(j == 0)
    def _():
        g = g_ref[...]

        def norm(rows):
            x = x_ref[rows, :]
            ms = jnp.mean(x * x, axis=-1, keepdims=True)
            h_scr[rows, :] = (x * lax.rsqrt(ms + EPS) * g).astype(BF16)
        _strips(TM_IN, norm, unroll=32)
        h = h_scr[...]
        gates = jnp.dot(h, wg_ref[...], preferred_element_type=F32) + bg_ref[...]
        gc_ref[...] = gates
        gr_ref[...] = gates.T[:GATE_ROWS, :]

    def run(epilogue, epi_cols=EPI_COLS, add_bias=True):
        assert sum(SUB_ROWS) == TM_IN
        for r0, n in zip(np.cumsum((0,) + SUB_ROWS[:-1]), SUB_ROWS):
            rows = pl.ds(int(r0), n)
            zb = jnp.dot(h_scr[rows, :], w_ref[...], preferred_element_type=F32)
            if add_bias:
                zb = zb + b_ref[...]
            for c0 in range(0, TN_IN, epi_cols):
                cols = slice(c0, c0 + epi_cols)
                z_ref[rows, cols] = epilogue(zb[:, cols], rows, cols).astype(BF16)

    is_conv = j < Z_MV // TN_IN
    is_rope = (j == Z_RQ // TN_IN) | (j == Z_RK // TN_IN)
    is_sigmoid = (j == Z_MO // TN_IN) | (j >= Z_GM // TN_IN)
    is_swish = (j >= Z_RG // TN_IN) & (j < Z_GM // TN_IN)
    is_plain = jnp.logical_not(is_conv | is_rope | is_sigmoid | is_swish)

    @pl.when(is_plain)
    def _():
        z = jnp.dot(h_scr[...], w_ref[...], preferred_element_type=F32) + b_ref[...]
        z_ref[...] = z.astype(BF16)

    @pl.when(is_sigmoid)
    def _():
        run(lambda zc, rows, cols: _sigmoid(zc))

    @pl.when(is_swish)
    def _():
        run(lambda zc, rows, cols: _swish(zc))

    @pl.when(is_rope)
    def _():
        scale = jnp.where(j == Z_RK // TN_IN, R_DK ** -0.5, 1.0)

        def rope(zc, rows, cols):
            c = cos_ref[rows, :] * scale
            s = sin_ref[rows, :] * scale
            x1, x2 = zc[:, :LANE], zc[:, LANE:]
            return jnp.concatenate([x1 * c - x2 * s, x1 * s + x2 * c], axis=1)
        run(rope)

    @pl.when(is_conv)
    def _():
        bias = b_ref[...]
        cw_half = cw_ref[...] * 0.5
        shift = jnp.sum(cw_half, axis=0, keepdims=True) * bias + cb_ref[...] * 0.5

        @pl.when(i % tiles_per_seq == 0)
        def _():
            carry_scr[j] = jnp.broadcast_to(-bias, carry_scr.shape[1:])
        scale = jnp.where(j == Z_MQ // TN_IN, M_DK ** -0.5, 1.0)

        row7 = lax.broadcasted_iota(jnp.int32, (8, CONV_COLS), 0) == 7

        def down_one(above, cur):
            return pltpu.roll(jnp.where(row7, above, cur), 1, axis=0)

        for r0, n in zip(np.cumsum((0,) + SUB_ROWS[:-1]), SUB_ROWS):
            r0 = int(r0)
            d = jnp.dot(h_scr[pl.ds(r0, n), :], w_ref[...], preferred_element_type=F32)
            for c0 in range(0, TN_IN, CONV_COLS):
                cols = slice(c0, c0 + CONV_COLS)
                w = [cw_half[tap:tap + 1, cols] for tap in range(CONV_K)]
                x = carry_scr[j, :, cols]
                above = [x * w[0]]
                for tap in (1, 2):
                    above.append(x * w[tap] + pltpu.roll(above[-1], 1, axis=0))
                done = []
                for p in range(0, n, 8):
                    x = d[p:p + 8, cols]
                    part = [x * w[0]]
                    for tap in (1, 2, 3):
                        part.append(x * w[tap] + down_one(above[tap - 1], part[-1]))
                    above = part[:3]
                    u = part[3] + shift[:, cols]
                    done.append((u * jnp.tanh(u) + u) * scale)
                    if len(done) == 2:
                        z_ref[pl.ds(r0 + p - 8, 16), cols] = jnp.concatenate(done, axis=0).astype(BF16)
                        done = []
                carry_scr[j, :, cols] = x


def _inproj(layer, seq, x2d, norm_g, w_z, b_z, w_g, b_g, conv_w, conv_b, cos, sin):
    T = x2d.shape[0]
    tiles_per_seq = seq // TM_IN
    n_conv = Z_MV // TN_IN
    grid = (T // TM_IN, Z_COLS // TN_IN)
    return pl.pallas_call(
        functools.partial(_inproj_kernel, tiles_per_seq=tiles_per_seq),
        grid=grid,
        in_specs=[
            pl.BlockSpec((TM_IN, D_MODEL), lambda i, j: (i, 0)),
            pl.BlockSpec((None, 1, D_MODEL), lambda i, j: (layer, 0, 0)),
            pl.BlockSpec((None, D_MODEL, TN_IN), lambda i, j: (layer, 0, j)),
            pl.BlockSpec((None, 1, TN_IN), lambda i, j: (layer, 0, j)),
            pl.BlockSpec((None, D_MODEL, GATE_LANES), lambda i, j: (layer, 0, 0)),
            pl.BlockSpec((None, 1, GATE_LANES), lambda i, j: (layer, 0, 0)),
            pl.BlockSpec((None, CONV_K, TN_IN), lambda i, j: (layer, 0, jnp.minimum(j, n_conv - 1))),
            pl.BlockSpec((None, 1, TN_IN), lambda i, j: (layer, 0, jnp.minimum(j, n_conv - 1))),
            pl.BlockSpec((TM_IN, LANE), lambda i, j: (i % tiles_per_seq, 0)),
            pl.BlockSpec((TM_IN, LANE), lambda i, j: (i % tiles_per_seq, 0)),
        ],
        out_specs=[
            pl.BlockSpec((TM_IN, TN_IN), lambda i, j: (i, j)),
            pl.BlockSpec((TM_IN, GATE_LANES), lambda i, j: (i, 0)),
            pl.BlockSpec((GATE_ROWS, TM_IN), lambda i, j: (0, i)),
        ],
        out_shape=[
            jax.ShapeDtypeStruct((T, Z_COLS), BF16),
            jax.ShapeDtypeStruct((T, GATE_LANES), F32),
            jax.ShapeDtypeStruct((GATE_ROWS, T), F32),
        ],
        scratch_shapes=[
            pltpu.VMEM((TM_IN, D_MODEL), BF16),
            pltpu.VMEM((n_conv, 8, TN_IN), F32),
        ],
        compiler_params=pltpu.CompilerParams(
            dimension_semantics=("arbitrary", "arbitrary"),
            vmem_limit_bytes=VMEM_LIMIT),
        name="inproj",
    )(x2d, norm_g, w_z, b_z, w_g, b_g, conv_w, conv_b, cos, sin)


def _mixer_kernel(q_ref, k_ref, v_ref, o_ref, rq_ref, rk_ref, rv_ref, rg_ref,
                  gc_ref, gr_ref, intra_ref, rcol_ref, mg_ref, rgain_ref,
                  ym_ref, yr_ref,
                  hm_scr, hr_scr, c_scr, r_scr, m_scr, *, cdec):
    L = CHUNK_L
    TS = TS_MIX
    t = pl.program_id(1)

    @pl.when(t == 0)
    def _():
        c_scr[...] = jnp.zeros(c_scr.shape, F32)
        r_scr[...] = jnp.zeros(r_scr.shape, F32)
        m_scr[...] = jnp.zeros(m_scr.shape, F32)

    row_id = lax.broadcasted_iota(jnp.int32, (L, L), 0)
    col_id = lax.broadcasted_iota(jnp.int32, (L, L), 1)
    causal = row_id >= col_id
    tril = causal.astype(BF16)
    triu = (row_id <= col_id).astype(BF16)
    ones_pad = jnp.ones((L, LANE), BF16)
    rcol = rcol_ref[...]

    for c in range(TS // L):
        rows = slice(c * L, (c + 1) * L)
        gcol = gc_ref[rows, :]
        grow = gr_ref[:, rows]
        b_cols = _dot01_left(tril, _log_sigmoid(gcol))
        b_rows = _dot01_right(_log_sigmoid(grow), triu)
        for h in range(HEADS):
            sl = slice(h * M_DK, (h + 1) * M_DK)
            q = q_ref[rows, sl]
            k = k_ref[rows, sl]
            vaug = jnp.concatenate([v_ref[rows, sl], ones_pad], axis=1)
            bcol = b_cols[:, HEADS + h:HEADS + h + 1]
            licol = gcol[:, h:h + 1]
            brow = b_rows[HEADS + h:HEADS + h + 1, :]
            lirow = grow[h:h + 1, :]
            m_prev = m_scr[h][0:1, 0:1]

            log_intra = jnp.where(causal, bcol - brow + lirow, -jnp.inf)
            inter = bcol + m_prev
            m_t = jnp.maximum(inter, jnp.max(log_intra, axis=-1, keepdims=True))
            dmat = jnp.exp(log_intra - m_t)
            d_inter = jnp.exp(inter - m_t)
            s = lax.dot_general(q, k, (((1,), (1,)), ((), ())), preferred_element_type=F32) * dmat
            c_old = c_scr[h]
            num = (jnp.dot(s.astype(BF16), vaug, preferred_element_type=F32)
                   + d_inter * jnp.dot(q, c_old.astype(BF16), preferred_element_type=F32))
            den = num[:, M_DV:M_DV + 1]
            scale = 1.0 / jnp.maximum(jnp.abs(den), jnp.exp(-m_t))
            hm_scr[rows, sl] = num[:, :M_DV] * scale

            b_end = bcol[L - 1:L, :]
            log_w = b_end - bcol + licol
            m_new = jnp.maximum(b_end + m_prev, jnp.max(log_w, axis=0, keepdims=True))
            w = jnp.exp(log_w - m_new)
            decay = jnp.exp(b_end + m_prev - m_new)
            kw = (k.astype(F32) * w).astype(BF16)
            c_scr[h] = decay * c_old + lax.dot_general(kw, vaug, (((0,), (0,)), ((), ())),
                                                       preferred_element_type=F32)
            m_scr[h] = jnp.broadcast_to(m_new, m_scr.shape[1:])

            sl = slice(h * R_DK, (h + 1) * R_DK)
            vs = slice(h * R_DV, (h + 1) * R_DV)
            rq = rq_ref[rows, sl]
            rk = rk_ref[rows, sl]
            rv = rv_ref[rows, vs]
            s = lax.dot_general(rq, rk, (((1,), (1,)), ((), ())), preferred_element_type=F32) * intra_ref[h]
            r_old = r_scr[h]
            hr_scr[rows, vs] = (jnp.dot(s.astype(BF16), rv, preferred_element_type=F32)
                                + jnp.dot(rq, r_old.astype(BF16), preferred_element_type=F32)
                                * rcol[:, h:h + 1])
            rkw = (rk.astype(F32) * rcol[:, HEADS + h:HEADS + h + 1]).astype(BF16)
            r_scr[h] = cdec[h] * r_old + lax.dot_general(rkw, rv, (((0,), (0,)), ((), ())),
                                                         preferred_element_type=F32)

    mg = mg_ref[...]
    rgain = rgain_ref[...]

    def head_norm(x):
        mu = jnp.mean(x, axis=-1, keepdims=True)
        xc = x - mu
        var = jnp.mean(xc * xc, axis=-1, keepdims=True)
        return xc * lax.rsqrt(var + EPS)

    def finish(rows):
        for h in range(HEADS):
            sl = slice(h * M_DV, (h + 1) * M_DV)
            y = o_ref[rows, sl].astype(F32) * (head_norm(hm_scr[rows, sl]) * mg[:, sl])
            ym_ref[rows, sl] = y.astype(BF16)
            vs = slice(h * R_DV, (h + 1) * R_DV)
            y = rg_ref[rows, vs].astype(F32) * (head_norm(hr_scr[rows, vs]) * rgain[:, vs])
            yr_ref[rows, vs] = y.astype(BF16)
    _strips(TS, finish, unroll=4)


def _retention_tables(L):
    gamma = 1.0 - np.exp2(-5.0 - np.arange(HEADS, dtype=np.float64))
    lg = np.log(gamma)
    pos = np.arange(L, dtype=np.float64)
    diff = pos[:, None] - pos[None, :]
    intra = np.where(diff >= 0, np.exp(np.maximum(diff, 0.0) * lg[:, None, None]), 0.0)
    rcol = np.zeros((L, LANE), np.float64)
    rcol[:, :HEADS] = np.exp((pos[:, None] + 1.0) * lg[None, :])
    rcol[:, HEADS:2 * HEADS] = np.exp((L - 1.0 - pos[:, None]) * lg[None, :])
    cdec = tuple(float(v) for v in np.exp(L * lg))
    return jnp.asarray(intra, F32), jnp.asarray(rcol, F32), cdec


def _mixer(layer, batch, seq, z, gcol, grow, m_gain, r_gain):
    L = CHUNK_L
    TS = TS_MIX
    nt = seq // TS
    intra, rcol, cdec = _retention_tables(L)

    def zspec(width, col0):
        blk = col0 // width
        return pl.BlockSpec((TS, width), lambda b, t: (b * nt + t, blk))

    def const(shape):
        return pl.BlockSpec(shape, lambda b, t: tuple(0 for _ in shape))

    def per_layer(width):
        return pl.BlockSpec((None, 1, width), lambda b, t: (layer, 0, 0))

    T = batch * seq
    return pl.pallas_call(
        functools.partial(_mixer_kernel, cdec=cdec),
        grid=(batch, nt),
        in_specs=[
            zspec(M_QK, Z_MQ), zspec(M_QK, Z_MK), zspec(M_V, Z_MV), zspec(M_V, Z_MO),
            zspec(R_QK, Z_RQ), zspec(R_QK, Z_RK), zspec(R_V, Z_RV), zspec(R_V, Z_RG),
            pl.BlockSpec((TS, GATE_LANES), lambda b, t: (b * nt + t, 0)),
            pl.BlockSpec((GATE_ROWS, TS), lambda b, t: (0, b * nt + t)),
            const((HEADS, L, L)), const((L, LANE)),
            per_layer(M_V), per_layer(R_V),
        ],
        out_specs=[
            pl.BlockSpec((TS, M_V), lambda b, t: (b * nt + t, 0)),
            pl.BlockSpec((TS, R_V), lambda b, t: (b * nt + t, 0)),
        ],
        out_shape=[
            jax.ShapeDtypeStruct((T, M_V), BF16),
            jax.ShapeDtypeStruct((T, R_V), BF16),
        ],
        scratch_shapes=[
            pltpu.VMEM((TS, M_V), F32), pltpu.VMEM((TS, R_V), F32),
            pltpu.VMEM((HEADS, M_DK, M_DV + LANE), F32),
            pltpu.VMEM((HEADS, R_DK, R_DV), F32),
            pltpu.VMEM((HEADS, 8, LANE), F32),
        ],
        compiler_params=pltpu.CompilerParams(
            dimension_semantics=("arbitrary", "arbitrary"),
            vmem_limit_bytes=VMEM_LIMIT),
        name="mixer",
    )(z, z, z, z, z, z, z, z, gcol, grow, intra, rcol, m_gain, r_gain)


def _rmsnorm_rows(src_ref, dst_ref, g, n_rows):
    def norm(rows):
        x = src_ref[rows, :]
        ms = jnp.mean(x * x, axis=-1, keepdims=True)
        dst_ref[rows, :] = (x * lax.rsqrt(ms + EPS) * g).astype(dst_ref.dtype)
    _strips(n_rows, norm, unroll=32)


def _post_kernel(ym_ref, yr_ref, gm_ref, gr_ref, x_ref, p_ref,
                 wbm_ref, wbr_ref, wout_ref, g2_ref, w1_ref, b1_ref, w2_ref, b2_ref,
                 g3_ref, wpg_ref, wpe_ref, gf_ref,
                 out_ref, x_scr, h_scr, *, final):
    a = jnp.dot(ym_ref[...], wbm_ref[...], preferred_element_type=F32)
    merged = gm_ref[...].astype(F32) * a
    b = jnp.dot(yr_ref[...], wbr_ref[...], preferred_element_type=F32)
    merged = merged + gr_ref[...].astype(F32) * b
    x_scr[...] = x_ref[...] + jnp.dot(merged.astype(BF16), wout_ref[...], preferred_element_type=F32)

    _rmsnorm_rows(x_scr, h_scr, g2_ref[...], TM_POST)
    h = h_scr[...]
    acc = x_scr[...] + b2_ref[...]
    for c in range(D_FF // D_MODEL):
        cs = slice(c * D_MODEL, (c + 1) * D_MODEL)
        u = jnp.maximum(jnp.dot(h, w1_ref[:, cs], preferred_element_type=F32) + b1_ref[:, cs], 0.0)
        acc = acc + jnp.dot((u * u).astype(BF16), w2_ref[cs, :], preferred_element_type=F32)
    x_scr[...] = acc

    _rmsnorm_rows(x_scr, h_scr, g3_ref[...], TM_POST)
    gate = _sigmoid(jnp.dot(h_scr[...], wpg_ref[...], preferred_element_type=F32))
    pe = jnp.dot(p_ref[...].astype(BF16), wpe_ref[...], preferred_element_type=F32)
    if final:
        x_scr[...] = x_scr[...] + gate * pe
        _rmsnorm_rows(x_scr, out_ref, gf_ref[...], TM_POST)
    else:
        out_ref[...] = x_scr[...] + gate * pe


def _post(layer, final, ym, yr, z, x2d, p3d, w_bm, w_br, w_out, g2, w1, b1, w2, b2, g3, wpg, wpe, gf):
    T = x2d.shape[0]
    tm = TM_POST

    def rows(width, blk=0):
        return pl.BlockSpec((tm, width), lambda i: (i, blk))

    def weight(k, n):
        return pl.BlockSpec((None, k, n), lambda i: (layer, 0, 0), pipeline_mode=pl.Buffered(1))

    def vec(n):
        return pl.BlockSpec((None, 1, n), lambda i: (layer, 0, 0))

    return pl.pallas_call(
        functools.partial(_post_kernel, final=final),
        grid=(T // tm,),
        in_specs=[
            rows(M_V), rows(R_V), rows(D_MODEL, Z_GM // D_MODEL), rows(D_MODEL, Z_GR // D_MODEL),
            rows(D_MODEL),
            pl.BlockSpec((None, tm, PE_DIM), lambda i: (layer, i, 0)),
            weight(M_V, D_MODEL), weight(R_V, D_MODEL), weight(D_MODEL, D_MODEL),
            vec(D_MODEL), weight(D_MODEL, D_FF), vec(D_FF), weight(D_FF, D_MODEL), vec(D_MODEL),
            vec(D_MODEL), weight(D_MODEL, D_MODEL), weight(PE_DIM, D_MODEL),
            pl.BlockSpec((1, D_MODEL), lambda i: (0, 0)),
        ],
        out_specs=rows(D_MODEL),
        out_shape=jax.ShapeDtypeStruct((T, D_MODEL), F32),
        scratch_shapes=[pltpu.VMEM((tm, D_MODEL), F32), pltpu.VMEM((tm, D_MODEL), BF16)],
        compiler_params=pltpu.CompilerParams(
            dimension_semantics=("arbitrary",),
            vmem_limit_bytes=VMEM_LIMIT),
        name="post_final" if final else "post",
    )(ym, yr, z, z, x2d, p3d, w_bm, w_br, w_out, g2, w1, b1, w2, b2, g3, wpg, wpe, gf)


def _reorder_in_proj(w_in, b_in):
    big = [(0, 4096), (4104, 4104 + 6144 + 2048)]
    w_z = jnp.concatenate([w_in[..., a:b].astype(BF16) for a, b in big], axis=-1)
    b_z = jnp.concatenate([b_in[..., a:b] for a, b in big], axis=-1)
    w_gate = w_in[..., 4096:4104]
    b_gate = b_in[..., 4096:4104]
    return w_z, b_z, w_gate, b_gate


def kernel(x, p, norm1_g, w_in, b_in, conv_w, conv_b, m_norm_g, r_norm_g, w_bm, w_br, w_out,
           norm2_g, w_ff1, b_ff1, w_ff2, b_ff2, norm3_g, w_pe_gate, w_pe, final_g):
    B, S, _ = x.shape
    depth = w_in.shape[0]
    T = B * S
    assert S % TS_MIX == 0 and TS_MIX % CHUNK_L == 0 and S % TM_IN == 0 and T % TM_POST == 0

    w_z, b_z, w_gate, b_gate = _reorder_in_proj(w_in, b_in)
    b_z = b_z[:, None, :]
    pad = GATE_LANES - GATE_ROWS
    w_g = jnp.pad(w_gate, ((0, 0), (0, 0), (0, pad))).astype(BF16)
    b_g = jnp.pad(b_gate, ((0, 0), (0, pad)))[:, None, :]

    pos = jnp.arange(S, dtype=F32)
    inv_freq = ROPE_BASE ** (-jnp.arange(0, R_DK, 2, dtype=F32) / R_DK)
    ang = pos[:, None] * inv_freq[None, :]
    cos = jnp.cos(ang)
    sin = jnp.sin(ang)

    bf = lambda w: w.astype(BF16)
    row = lambda v: v[:, None, :]
    w_bm, w_br, w_out, w_ff1, w_ff2, w_pe_gate, w_pe = map(
        bf, (w_bm, w_br, w_out, w_ff1, w_ff2, w_pe_gate, w_pe))
    norm1_g, conv_b, m_norm_g, r_norm_g, norm2_g, b_ff1, b_ff2, norm3_g = map(
        row, (norm1_g, conv_b, m_norm_g, r_norm_g, norm2_g, b_ff1, b_ff2, norm3_g))
    final_g = final_g[None, :]

    x2d = x.reshape(T, D_MODEL)
    p3d = p.reshape(depth, T, PE_DIM)
    for i in range(depth):
        z, gcol, grow = _inproj(i, S, x2d, norm1_g, w_z, b_z, w_g, b_g,
                                conv_w, conv_b, cos, sin)
        ym, yr = _mixer(i, B, S, z, gcol, grow, m_norm_g, r_norm_g)
        x2d = _post(i, i == depth - 1, ym, yr, z, x2d, p3d, w_bm, w_br, w_out,
                    norm2_g, w_ff1, b_ff1, w_ff2, b_ff2, norm3_g, w_pe_gate, w_pe, final_g)
    return x2d.reshape(B, S, D_MODEL)
```

```python
import functools

import numpy as np
import jax
import jax.numpy as jnp
from jax import lax
from jax.experimental import pallas as pl
from jax.experimental.pallas import tpu as pltpu

F32 = jnp.float32
BF16 = jnp.bfloat16

D_MODEL = 1024
PE_DIM = 256
EPS = 1e-6
HEADS = 4
M_DK = 256
M_DV = 256
R_DK = 256
R_DV = 512
M_QK = HEADS * M_DK
M_V = HEADS * M_DV
R_QK = HEADS * R_DK
R_V = HEADS * R_DV
CONV_K = 4
ROPE_BASE = 10000.0
D_FF = 4 * D_MODEL

Z_MQ, Z_MK, Z_MV, Z_MO = 0, 1024, 2048, 3072
Z_RQ, Z_RK, Z_RV, Z_RG = 4096, 5120, 6144, 8192
Z_GM, Z_GR = 10240, 11264
Z_COLS = 12288
GATE_LANES = 128
GATE_ROWS = 8

LANE = 128
VMEM_LIMIT = 60 * 1024 * 1024

TM_IN = 2048
TN_IN = 1024
SUB_ROWS = (512, 512, 512, 512)
EPI_COLS = 256
CONV_COLS = 512
CHUNK_L = 256
TS_MIX = 256
TM_POST = 512
STRIP = 16


def _sigmoid(x):
    return 0.5 * jnp.tanh(0.5 * x) + 0.5


def _swish(x):
    u = 0.5 * x
    return u * jnp.tanh(u) + u


def _log_sigmoid(x):
    return jnp.minimum(x, 0.0) - jnp.log1p(jnp.exp(-jnp.abs(x)))


def _split3(x):
    hi = x.astype(BF16)
    r1 = x - hi.astype(F32)
    mid = r1.astype(BF16)
    lo = (r1 - mid.astype(F32)).astype(BF16)
    return hi, mid, lo


def _dot01_left(mat01, x):
    return sum(jnp.dot(mat01, piece, preferred_element_type=F32) for piece in _split3(x))


def _dot01_right(x, mat01):
    return sum(jnp.dot(piece, mat01, preferred_element_type=F32) for piece in _split3(x))


def _strips(n_rows, body, rows_per_strip=STRIP, unroll=1):
    def step(i, carry):
        body(pl.ds(pl.multiple_of(i * rows_per_strip, rows_per_strip), rows_per_strip))
        return carry
    lax.fori_loop(0, n_rows // rows_per_strip, step, 0, unroll=unroll)


def _inproj_kernel(x_ref, g_ref, w_ref, b_ref, wg_ref, bg_ref,
                   cw_ref, cb_ref, cos_ref, sin_ref,
                   z_ref, gc_ref, gr_ref, h_scr, carry_scr, *, tiles_per_seq):
    i = pl.program_id(0)
    j = pl.program_id(1)

    @pl.when(j == 0)
    def _():
        g = g_ref[...]

        def norm(rows):
            x = x_ref[rows, :]
            ms = jnp.mean(x * x, axis=-1, keepdims=True)
            h_scr[rows, :] = (x * lax.rsqrt(ms + EPS) * g).astype(BF16)
        _strips(TM_IN, norm, unroll=32)
        h = h_scr[...]
        gates = jnp.dot(h, wg_ref[...], preferred_element_type=F32) + bg_ref[...]
        gc_ref[...] = gates
        gr_ref[...] = gates.T[:GATE_ROWS, :]

    def run(epilogue, epi_cols=EPI_COLS, add_bias=True):
        assert sum(SUB_ROWS) == TM_IN
        for r0, n in zip(np.cumsum((0,) + SUB_ROWS[:-1]), SUB_ROWS):
            rows = pl.ds(int(r0), n)
            zb = jnp.dot(h_scr[rows, :], w_ref[...], preferred_element_type=F32)
            if add_bias:
                zb = zb + b_ref[...]
            for c0 in range(0, TN_IN, epi_cols):
                cols = slice(c0, c0 + epi_cols)
                z_ref[rows, cols] = epilogue(zb[:, cols], rows, cols).astype(BF16)

    is_conv = j < Z_MV // TN_IN
    is_rope = (j == Z_RQ // TN_IN) | (j == Z_RK // TN_IN)
    is_sigmoid = (j == Z_MO // TN_IN) | (j >= Z_GM // TN_IN)
    is_swish = (j >= Z_RG // TN_IN) & (j < Z_GM // TN_IN)
    is_plain = jnp.logical_not(is_conv | is_rope | is_sigmoid | is_swish)

    @pl.when(is_plain)
    def _():
        z = jnp.dot(h_scr[...], w_ref[...], preferred_element_type=F32) + b_ref[...]
        z_ref[...] = z.astype(BF16)

    @pl.when(is_sigmoid)
    def _():
        run(lambda zc, rows, cols: _sigmoid(zc))

    @pl.when(is_swish)
    def _():
        run(lambda zc, rows, cols: _swish(zc))

    @pl.when(is_rope)
    def _():
        scale = jnp.where(j == Z_RK // TN_IN, R_DK ** -0.5, 1.0)

        def rope(zc, rows, cols):
            c = cos_ref[rows, :] * scale
            s = sin_ref[rows, :] * scale
            x1, x2 = zc[:, :LANE], zc[:, LANE:]
            return jnp.concatenate([x1 * c - x2 * s, x1 * s + x2 * c], axis=1)
        run(rope)

    @pl.when(is_conv)
    def _():
        bias = b_ref[...]
        cw_half = cw_ref[...] * 0.5
        shift = jnp.sum(cw_half, axis=0, keepdims=True) * bias + cb_ref[...] * 0.5

        @pl.when(i % tiles_per_seq == 0)
        def _():
            carry_scr[j] = jnp.broadcast_to(-bias, carry_scr.shape[1:])
        scale = jnp.where(j == Z_MQ // TN_IN, M_DK ** -0.5, 1.0)

        row7 = lax.broadcasted_iota(jnp.int32, (8, CONV_COLS), 0) == 7

        def down_one(above, cur):
            return pltpu.roll(jnp.where(row7, above, cur), 1, axis=0)

        for r0, n in zip(np.cumsum((0,) + SUB_ROWS[:-1]), SUB_ROWS):
            r0 = int(r0)
            d = jnp.dot(h_scr[pl.ds(r0, n), :], w_ref[...], preferred_element_type=F32)
            for c0 in range(0, TN_IN, CONV_COLS):
                cols = slice(c0, c0 + CONV_COLS)
                w = [cw_half[tap:tap + 1, cols] for tap in range(CONV_K)]
                x = carry_scr[j, :, cols]
                above = [x * w[0]]
                for tap in (1, 2):
                    above.append(x * w[tap] + pltpu.roll(above[-1], 1, axis=0))
                done = []
                for p in range(0, n, 8):
                    x = d[p:p + 8, cols]
                    part = [x * w[0]]
                    for tap in (1, 2, 3):
                        part.append(x * w[tap] + down_one(above[tap - 1], part[-1]))
                    above = part[:3]
                    u = part[3] + shift[:, cols]
                    done.append((u * jnp.tanh(u) + u) * scale)
                    if len(done) == 2:
                        z_ref[pl.ds(r0 + p - 8, 16), cols] = jnp.concatenate(done, axis=0).astype(BF16)
                        done = []
                carry_scr[j, :, cols] = x


def _inproj(layer, seq, x2d, norm_g, w_z, b_z, w_g, b_g, conv_w, conv_b, cos, sin):
    T = x2d.shape[0]
    tiles_per_seq = seq // TM_IN
    n_conv = Z_MV // TN_IN
    grid = (T // TM_IN, Z_COLS // TN_IN)
    return pl.pallas_call(
        functools.partial(_inproj_kernel, tiles_per_seq=tiles_per_seq),
        grid=grid,
        in_specs=[
            pl.BlockSpec((TM_IN, D_MODEL), lambda i, j: (i, 0)),
            pl.BlockSpec((None, 1, D_MODEL), lambda i, j: (layer, 0, 0)),
            pl.BlockSpec((None, D_MODEL, TN_IN), lambda i, j: (layer, 0, j)),
            pl.BlockSpec((None, 1, TN_IN), lambda i, j: (layer, 0, j)),
            pl.BlockSpec((None, D_MODEL, GATE_LANES), lambda i, j: (layer, 0, 0)),
            pl.BlockSpec((None, 1, GATE_LANES), lambda i, j: (layer, 0, 0)),
            pl.BlockSpec((None, CONV_K, TN_IN), lambda i, j: (layer, 0, jnp.minimum(j, n_conv - 1))),
            pl.BlockSpec((None, 1, TN_IN), lambda i, j: (layer, 0, jnp.minimum(j, n_conv - 1))),
            pl.BlockSpec((TM_IN, LANE), lambda i, j: (i % tiles_per_seq, 0)),
            pl.BlockSpec((TM_IN, LANE), lambda i, j: (i % tiles_per_seq, 0)),
        ],
        out_specs=[
            pl.BlockSpec((TM_IN, TN_IN), lambda i, j: (i, j)),
            pl.BlockSpec((TM_IN, GATE_LANES), lambda i, j: (i, 0)),
            pl.BlockSpec((GATE_ROWS, TM_IN), lambda i, j: (0, i)),
        ],
        out_shape=[
            jax.ShapeDtypeStruct((T, Z_COLS), BF16),
            jax.ShapeDtypeStruct((T, GATE_LANES), F32),
            jax.ShapeDtypeStruct((GATE_ROWS, T), F32),
        ],
        scratch_shapes=[
            pltpu.VMEM((TM_IN, D_MODEL), BF16),
            pltpu.VMEM((n_conv, 8, TN_IN), F32),
        ],
        compiler_params=pltpu.CompilerParams(
            dimension_semantics=("arbitrary", "arbitrary"),
            vmem_limit_bytes=VMEM_LIMIT),
        name="inproj",
    )(x2d, norm_g, w_z, b_z, w_g, b_g, conv_w, conv_b, cos, sin)


def _mixer_kernel(q_ref, k_ref, v_ref, o_ref, rq_ref, rk_ref, rv_ref, rg_ref,
                  gc_ref, gr_ref, intra_ref, rcol_ref, mg_ref, rgain_ref,
                  ym_ref, yr_ref,
                  hm_scr, hr_scr, c_scr, r_scr, m_scr, *, cdec):
    L = CHUNK_L
    TS = TS_MIX
    t = pl.program_id(1)

    @pl.when(t == 0)
    def _():
        c_scr[...] = jnp.zeros(c_scr.shape, F32)
        r_scr[...] = jnp.zeros(r_scr.shape, F32)
        m_scr[...] = jnp.zeros(m_scr.shape, F32)

    row_id = lax.broadcasted_iota(jnp.int32, (L, L), 0)
    col_id = lax.broadcasted_iota(jnp.int32, (L, L), 1)
    causal = row_id >= col_id
    tril = causal.astype(BF16)
    triu = (row_id <= col_id).astype(BF16)
    ones_pad = jnp.ones((L, LANE), BF16)
    rcol = rcol_ref[...]

    for c in range(TS // L):
        rows = slice(c * L, (c + 1) * L)
        gcol = gc_ref[rows, :]
        grow = gr_ref[:, rows]
        b_cols = _dot01_left(tril, _log_sigmoid(gcol))
        b_rows = _dot01_right(_log_sigmoid(grow), triu)
        for h in range(HEADS):
            sl = slice(h * M_DK, (h + 1) * M_DK)
            q = q_ref[rows, sl]
            k = k_ref[rows, sl]
            vaug = jnp.concatenate([v_ref[rows, sl], ones_pad], axis=1)
            bcol = b_cols[:, HEADS + h:HEADS + h + 1]
            licol = gcol[:, h:h + 1]
            brow = b_rows[HEADS + h:HEADS + h + 1, :]
            lirow = grow[h:h + 1, :]
            m_prev = m_scr[h][0:1, 0:1]

            log_intra = jnp.where(causal, bcol - brow + lirow, -jnp.inf)
            inter = bcol + m_prev
            m_t = jnp.maximum(inter, jnp.max(log_intra, axis=-1, keepdims=True))
            dmat = jnp.exp(log_intra - m_t)
            d_inter = jnp.exp(inter - m_t)
            s = lax.dot_general(q, k, (((1,), (1,)), ((), ())), preferred_element_type=F32) * dmat
            c_old = c_scr[h]
            num = (jnp.dot(s.astype(BF16), vaug, preferred_element_type=F32)
                   + d_inter * jnp.dot(q, c_old.astype(BF16), preferred_element_type=F32))
            den = num[:, M_DV:M_DV + 1]
            scale = 1.0 / jnp.maximum(jnp.abs(den), jnp.exp(-m_t))
            hm_scr[rows, sl] = num[:, :M_DV] * scale

            b_end = bcol[L - 1:L, :]
            log_w = b_end - bcol + licol
            m_new = jnp.maximum(b_end + m_prev, jnp.max(log_w, axis=0, keepdims=True))
            w = jnp.exp(log_w - m_new)
            decay = jnp.exp(b_end + m_prev - m_new)
            kw = (k.astype(F32) * w).astype(BF16)
            c_scr[h] = decay * c_old + lax.dot_general(kw, vaug, (((0,), (0,)), ((), ())),
                                                       preferred_element_type=F32)
            m_scr[h] = jnp.broadcast_to(m_new, m_scr.shape[1:])

            sl = slice(h * R_DK, (h + 1) * R_DK)
            vs = slice(h * R_DV, (h + 1) * R_DV)
            rq = rq_ref[rows, sl]
            rk = rk_ref[rows, sl]
            rv = rv_ref[rows, vs]
            s = lax.dot_general(rq, rk, (((1,), (1,)), ((), ())), preferred_element_type=F32) * intra_ref[h]
            r_old = r_scr[h]
            hr_scr[rows, vs] = (jnp.dot(s.astype(BF16), rv, preferred_element_type=F32)
                                + jnp.dot(rq, r_old.astype(BF16), preferred_element_type=F32)
                                * rcol[:, h:h + 1])
            rkw = (rk.astype(F32) * rcol[:, HEADS + h:HEADS + h + 1]).astype(BF16)
            r_scr[h] = cdec[h] * r_old + lax.dot_general(rkw, rv, (((0,), (0,)), ((), ())),
                                                         preferred_element_type=F32)

    mg = mg_ref[...]
    rgain = rgain_ref[...]

    def head_norm(x):
        mu = jnp.mean(x, axis=-1, keepdims=True)
        xc = x - mu
        var = jnp.mean(xc * xc, axis=-1, keepdims=True)
        return xc * lax.rsqrt(var + EPS)

    def finish(rows):
        for h in range(HEADS):
            sl = slice(h * M_DV, (h + 1) * M_DV)
            y = o_ref[rows, sl].astype(F32) * (head_norm(hm_scr[rows, sl]) * mg[:, sl])
            ym_ref[rows, sl] = y.astype(BF16)
            vs = slice(h * R_DV, (h + 1) * R_DV)
            y = rg_ref[rows, vs].astype(F32) * (head_norm(hr_scr[rows, vs]) * rgain[:, vs])
            yr_ref[rows, vs] = y.astype(BF16)
    _strips(TS, finish, unroll=4)


def _retention_tables(L):
    gamma = 1.0 - np.exp2(-5.0 - np.arange(HEADS, dtype=np.float64))
    lg = np.log(gamma)
    pos = np.arange(L, dtype=np.float64)
    diff = pos[:, None] - pos[None, :]
    intra = np.where(diff >= 0, np.exp(np.maximum(diff, 0.0) * lg[:, None, None]), 0.0)
    rcol = np.zeros((L, LANE), np.float64)
    rcol[:, :HEADS] = np.exp((pos[:, None] + 1.0) * lg[None, :])
    rcol[:, HEADS:2 * HEADS] = np.exp((L - 1.0 - pos[:, None]) * lg[None, :])
    cdec = tuple(float(v) for v in np.exp(L * lg))
    return jnp.asarray(intra, F32), jnp.asarray(rcol, F32), cdec


def _mixer(layer, batch, seq, z, gcol, grow, m_gain, r_gain):
    L = CHUNK_L
    TS = TS_MIX
    nt = seq // TS
    intra, rcol, cdec = _retention_tables(L)

    def zspec(width, col0):
        blk = col0 // width
        return pl.BlockSpec((TS, width), lambda b, t: (b * nt + t, blk))

    def const(shape):
        return pl.BlockSpec(shape, lambda b, t: tuple(0 for _ in shape))

    def per_layer(width):
        return pl.BlockSpec((None, 1, width), lambda b, t: (layer, 0, 0))

    T = batch * seq
    return pl.pallas_call(
        functools.partial(_mixer_kernel, cdec=cdec),
        grid=(batch, nt),
        in_specs=[
            zspec(M_QK, Z_MQ), zspec(M_QK, Z_MK), zspec(M_V, Z_MV), zspec(M_V, Z_MO),
            zspec(R_QK, Z_RQ), zspec(R_QK, Z_RK), zspec(R_V, Z_RV), zspec(R_V, Z_RG),
            pl.BlockSpec((TS, GATE_LANES), lambda b, t: (b * nt + t, 0)),
            pl.BlockSpec((GATE_ROWS, TS), lambda b, t: (0, b * nt + t)),
            const((HEADS, L, L)), const((L, LANE)),
            per_layer(M_V), per_layer(R_V),
        ],
        out_specs=[
            pl.BlockSpec((TS, M_V), lambda b, t: (b * nt + t, 0)),
            pl.BlockSpec((TS, R_V), lambda b, t: (b * nt + t, 0)),
        ],
        out_shape=[
            jax.ShapeDtypeStruct((T, M_V), BF16),
            jax.ShapeDtypeStruct((T, R_V), BF16),
        ],
        scratch_shapes=[
            pltpu.VMEM((TS, M_V), F32), pltpu.VMEM((TS, R_V), F32),
            pltpu.VMEM((HEADS, M_DK, M_DV + LANE), F32),
            pltpu.VMEM((HEADS, R_DK, R_DV), F32),
            pltpu.VMEM((HEADS, 8, LANE), F32),
        ],
        compiler_params=pltpu.CompilerParams(
            dimension_semantics=("arbitrary", "arbitrary"),
            vmem_limit_bytes=VMEM_LIMIT),
        name="mixer",
    )(z, z, z, z, z, z, z, z, gcol, grow, intra, rcol, m_gain, r_gain)


def _rmsnorm_rows(src_ref, dst_ref, g, n_rows):
    def norm(rows):
        x = src_ref[rows, :]
        ms = jnp.mean(x * x, axis=-1, keepdims=True)
        dst_ref[rows, :] = (x * lax.rsqrt(ms + EPS) * g).astype(dst_ref.dtype)
    _strips(n_rows, norm, unroll=32)


def _post_kernel(ym_ref, yr_ref, gm_ref, gr_ref, x_ref, p_ref,
                 wbm_ref, wbr_ref, wout_ref, g2_ref, w1_ref, b1_ref, w2_ref, b2_ref,
                 g3_ref, wpg_ref, wpe_ref, gf_ref,
                 out_ref, x_scr, h_scr, *, final):
    a = jnp.dot(ym_ref[...], wbm_ref[...], preferred_element_type=F32)
    merged = gm_ref[...].astype(F32) * a
    b = jnp.dot(yr_ref[...], wbr_ref[...], preferred_element_type=F32)
    merged = merged + gr_ref[...].astype(F32) * b
    x_scr[...] = x_ref[...] + jnp.dot(merged.astype(BF16), wout_ref[...], preferred_element_type=F32)

    _rmsnorm_rows(x_scr, h_scr, g2_ref[...], TM_POST)
    h = h_scr[...]
    acc = x_scr[...] + b2_ref[...]
    for c in range(D_FF // D_MODEL):
        cs = slice(c * D_MODEL, (c + 1) * D_MODEL)
        u = jnp.maximum(jnp.dot(h, w1_ref[:, cs], preferred_element_type=F32) + b1_ref[:, cs], 0.0)
        acc = acc + jnp.dot((u * u).astype(BF16), w2_ref[cs, :], preferred_element_type=F32)
    x_scr[...] = acc

    _rmsnorm_rows(x_scr, h_scr, g3_ref[...], TM_POST)
    gate = _sigmoid(jnp.dot(h_scr[...], wpg_ref[...], preferred_element_type=F32))
    pe = jnp.dot(p_ref[...].astype(BF16), wpe_ref[...], preferred_element_type=F32)
    if final:
        x_scr[...] = x_scr[...] + gate * pe
        _rmsnorm_rows(x_scr, out_ref, gf_ref[...], TM_POST)
    else:
        out_ref[...] = x_scr[...] + gate * pe


def _post(layer, final, ym, yr, z, x2d, p3d, w_bm, w_br, w_out, g2, w1, b1, w2, b2, g3, wpg, wpe, gf):
    T = x2d.shape[0]
    tm = TM_POST

    def rows(width, blk=0):
        return pl.BlockSpec((tm, width), lambda i: (i, blk))

    def weight(k, n):
        return pl.BlockSpec((None, k, n), lambda i: (layer, 0, 0), pipeline_mode=pl.Buffered(1))

    def vec(n):
        return pl.BlockSpec((None, 1, n), lambda i: (layer, 0, 0))

    return pl.pallas_call(
        functools.partial(_post_kernel, final=final),
        grid=(T // tm,),
        in_specs=[
            rows(M_V), rows(R_V), rows(D_MODEL, Z_GM // D_MODEL), rows(D_MODEL, Z_GR // D_MODEL),
            rows(D_MODEL),
            pl.BlockSpec((None, tm, PE_DIM), lambda i: (layer, i, 0)),
            weight(M_V, D_MODEL), weight(R_V, D_MODEL), weight(D_MODEL, D_MODEL),
            vec(D_MODEL), weight(D_MODEL, D_FF), vec(D_FF), weight(D_FF, D_MODEL), vec(D_MODEL),
            vec(D_MODEL), weight(D_MODEL, D_MODEL), weight(PE_DIM, D_MODEL),
            pl.BlockSpec((1, D_MODEL), lambda i: (0, 0)),
        ],
        out_specs=rows(D_MODEL),
        out_shape=jax.ShapeDtypeStruct((T, D_MODEL), F32),
        scratch_shapes=[pltpu.VMEM((tm, D_MODEL), F32), pltpu.VMEM((tm, D_MODEL), BF16)],
        compiler_params=pltpu.CompilerParams(
            dimension_semantics=("arbitrary",),
            vmem_limit_bytes=VMEM_LIMIT),
        name="post_final" if final else "post",
    )(ym, yr, z, z, x2d, p3d, w_bm, w_br, w_out, g2, w1, b1, w2, b2, g3, wpg, wpe, gf)


def _reorder_in_proj(w_in, b_in):
    big = [(0, 4096), (4104, 4104 + 6144 + 2048)]
    w16 = w_in.astype(BF16)
    w_z = lax.dynamic_update_slice(w16[..., big[1][0] - big[0][1]:], w16[..., :big[0][1]], (0, 0, 0))
    b_z = jnp.concatenate([b_in[..., a:b] for a, b in big], axis=-1)
    w_gate = w_in[..., 4096:4104]
    b_gate = b_in[..., 4096:4104]
    return w_z, b_z, w_gate, b_gate


def kernel(x, p, norm1_g, w_in, b_in, conv_w, conv_b, m_norm_g, r_norm_g, w_bm, w_br, w_out,
           norm2_g, w_ff1, b_ff1, w_ff2, b_ff2, norm3_g, w_pe_gate, w_pe, final_g):
    B, S, _ = x.shape
    depth = w_in.shape[0]
    T = B * S
    assert S % TS_MIX == 0 and TS_MIX % CHUNK_L == 0 and S % TM_IN == 0 and T % TM_POST == 0

    w_z, b_z, w_gate, b_gate = _reorder_in_proj(w_in, b_in)
    b_z = b_z[:, None, :]
    pad = GATE_LANES - GATE_ROWS
    w_g = jnp.pad(w_gate, ((0, 0), (0, 0), (0, pad))).astype(BF16)
    b_g = jnp.pad(b_gate, ((0, 0), (0, pad)))[:, None, :]

    pos = jnp.arange(S, dtype=F32)
    inv_freq = ROPE_BASE ** (-jnp.arange(0, R_DK, 2, dtype=F32) / R_DK)
    ang = pos[:, None] * inv_freq[None, :]
    cos = jnp.cos(ang)
    sin = jnp.sin(ang)

    bf = lambda w: w.astype(BF16)
    row = lambda v: v[:, None, :]
    w_bm, w_br, w_out, w_ff1, w_ff2, w_pe_gate, w_pe = map(
        bf, (w_bm, w_br, w_out, w_ff1, w_ff2, w_pe_gate, w_pe))
    norm1_g, conv_b, m_norm_g, r_norm_g, norm2_g, b_ff1, b_ff2, norm3_g = map(
        row, (norm1_g, conv_b, m_norm_g, r_norm_g, norm2_g, b_ff1, b_ff2, norm3_g))
    final_g = final_g[None, :]

    x2d = x.reshape(T, D_MODEL)
    p3d = p.reshape(depth, T, PE_DIM)
    for i in range(depth):
        z, gcol, grow = _inproj(i, S, x2d, norm1_g, w_z, b_z, w_g, b_g,
                                conv_w, conv_b, cos, sin)
        ym, yr = _mixer(i, B, S, z, gcol, grow, m_norm_g, r_norm_g)
        x2d = _post(i, i == depth - 1, ym, yr, z, x2d, p3d, w_bm, w_br, w_out,
                    norm2_g, w_ff1, b_ff1, w_ff2, b_ff2, norm3_g, w_pe_gate, w_pe, final_g)
    return x2d.reshape(B, S, D_MODEL)
```

```python
import functools

import numpy as np
import jax
import jax.numpy as jnp
from jax import lax
from jax.experimental import pallas as pl
from jax.experimental.pallas import tpu as pltpu

F32 = jnp.float32
BF16 = jnp.bfloat16

D_MODEL = 1024
PE_DIM = 256
EPS = 1e-6
HEADS = 4
M_DK = 256
M_DV = 256
R_DK = 256
R_DV = 512
M_QK = HEADS * M_DK
M_V = HEADS * M_DV
R_QK = HEADS * R_DK
R_V = HEADS * R_DV
CONV_K = 4
ROPE_BASE = 10000.0
D_FF = 4 * D_MODEL

Z_MQ, Z_MK, Z_MV, Z_MO = 0, 1024, 2048, 3072
Z_RQ, Z_RK, Z_RV, Z_RG = 4096, 5120, 6144, 8192
Z_GM, Z_GR = 10240, 11264
Z_COLS = 12288
GATE_LANES = 128
GATE_ROWS = 8

LANE = 128
VMEM_LIMIT = 60 * 1024 * 1024

TM_IN = 2048
TN_IN = 1024
SUB_ROWS = (512, 512, 512, 512)
EPI_COLS = 256
CONV_COLS = 512
CHUNK_L = 256
TS_MIX = 256
TM_POST = 512
STRIP = 16


def _sigmoid(x):
    return 0.5 * jnp.tanh(0.5 * x) + 0.5


def _swish(x):
    u = 0.5 * x
    return u * jnp.tanh(u) + u


def _log_sigmoid(x):
    return jnp.minimum(x, 0.0) - jnp.log1p(jnp.exp(-jnp.abs(x)))


def _split3(x):
    hi = x.astype(BF16)
    r1 = x - hi.astype(F32)
    mid = r1.astype(BF16)
    lo = (r1 - mid.astype(F32)).astype(BF16)
    return hi, mid, lo


def _dot01_left(mat01, x):
    return sum(jnp.dot(mat01, piece, preferred_element_type=F32) for piece in _split3(x))


def _dot01_right(x, mat01):
    return sum(jnp.dot(piece, mat01, preferred_element_type=F32) for piece in _split3(x))


def _strips(n_rows, body, rows_per_strip=STRIP, unroll=1):
    def step(i, carry):
        body(pl.ds(pl.multiple_of(i * rows_per_strip, rows_per_strip), rows_per_strip))
        return carry
    lax.fori_loop(0, n_rows // rows_per_strip, step, 0, unroll=unroll)


def _inproj_kernel(x_ref, g_ref, w_ref, b_ref, wg_ref, bg_ref,
                   cw_ref, cb_ref, cos_ref, sin_ref,
                   z_ref, gc_ref, gr_ref, h_scr, carry_scr, *, tiles_per_seq):
    i = pl.program_id(0)
    j = pl.program_id(1)

    @pl.when(j == 0)
    def _():
        g = g_ref[...]

        def norm(rows):
            x = x_ref[rows, :]
            ms = jnp.mean(x * x, axis=-1, keepdims=True)
            h_scr[rows, :] = (x * lax.rsqrt(ms + EPS) * g).astype(BF16)
        _strips(TM_IN, norm, unroll=32)
        h = h_scr[...]
        gates = jnp.dot(h, wg_ref[...], preferred_element_type=F32) + bg_ref[...]
        gc_ref[...] = gates
        gr_ref[...] = gates.T[:GATE_ROWS, :]

    def run(epilogue, epi_cols=EPI_COLS, add_bias=True):
        assert sum(SUB_ROWS) == TM_IN
        for r0, n in zip(np.cumsum((0,) + SUB_ROWS[:-1]), SUB_ROWS):
            rows = pl.ds(int(r0), n)
            zb = jnp.dot(h_scr[rows, :], w_ref[...], preferred_element_type=F32)
            if add_bias:
                zb = zb + b_ref[...]
            for c0 in range(0, TN_IN, epi_cols):
                cols = slice(c0, c0 + epi_cols)
                z_ref[rows, cols] = epilogue(zb[:, cols], rows, cols).astype(BF16)

    is_conv = j < Z_MV // TN_IN
    is_rope = (j == Z_RQ // TN_IN) | (j == Z_RK // TN_IN)
    is_sigmoid = (j == Z_MO // TN_IN) | (j >= Z_GM // TN_IN)
    is_swish = (j >= Z_RG // TN_IN) & (j < Z_GM // TN_IN)
    is_plain = jnp.logical_not(is_conv | is_rope | is_sigmoid | is_swish)

    @pl.when(is_plain)
    def _():
        z = jnp.dot(h_scr[...], w_ref[...], preferred_element_type=F32) + b_ref[...]
        z_ref[...] = z.astype(BF16)

    @pl.when(is_sigmoid)
    def _():
        run(lambda zc, rows, cols: _sigmoid(zc))

    @pl.when(is_swish)
    def _():
        run(lambda zc, rows, cols: _swish(zc))

    @pl.when(is_rope)
    def _():
        scale = jnp.where(j == Z_RK // TN_IN, R_DK ** -0.5, 1.0)

        def rope(zc, rows, cols):
            c = cos_ref[rows, :] * scale
            s = sin_ref[rows, :] * scale
            x1, x2 = zc[:, :LANE], zc[:, LANE:]
            return jnp.concatenate([x1 * c - x2 * s, x1 * s + x2 * c], axis=1)
        run(rope)

    @pl.when(is_conv)
    def _():
        bias = b_ref[...]
        cw_half = cw_ref[...] * 0.5
        shift = jnp.sum(cw_half, axis=0, keepdims=True) * bias + cb_ref[...] * 0.5

        @pl.when(i % tiles_per_seq == 0)
        def _():
            carry_scr[j] = jnp.broadcast_to(-bias, carry_scr.shape[1:])
        scale = jnp.where(j == Z_MQ // TN_IN, M_DK ** -0.5, 1.0)

        row7 = lax.broadcasted_iota(jnp.int32, (8, CONV_COLS), 0) == 7

        def down_one(above, cur):
            return pltpu.roll(jnp.where(row7, above, cur), 1, axis=0)

        for r0, n in zip(np.cumsum((0,) + SUB_ROWS[:-1]), SUB_ROWS):
            r0 = int(r0)
            d = jnp.dot(h_scr[pl.ds(r0, n), :], w_ref[...], preferred_element_type=F32)
            for c0 in range(0, TN_IN, CONV_COLS):
                cols = slice(c0, c0 + CONV_COLS)
                w = [cw_half[tap:tap + 1, cols] for tap in range(CONV_K)]
                x = carry_scr[j, :, cols]
                above = [x * w[0]]
                for tap in (1, 2):
                    above.append(x * w[tap] + pltpu.roll(above[-1], 1, axis=0))
                done = []
                for p in range(0, n, 8):
                    x = d[p:p + 8, cols]
                    part = [x * w[0]]
                    for tap in (1, 2, 3):
                        part.append(x * w[tap] + down_one(above[tap - 1], part[-1]))
                    above = part[:3]
                    u = part[3] + shift[:, cols]
                    done.append((u * jnp.tanh(u) + u) * scale)
                    if len(done) == 2:
                        z_ref[pl.ds(r0 + p - 8, 16), cols] = jnp.concatenate(done, axis=0).astype(BF16)
                        done = []
                carry_scr[j, :, cols] = x


def _inproj(layer, seq, x2d, norm_g, w_z, b_z, w_g, b_g, conv_w, conv_b, cos, sin):
    T = x2d.shape[0]
    tiles_per_seq = seq // TM_IN
    n_conv = Z_MV // TN_IN
    grid = (T // TM_IN, Z_COLS // TN_IN)
    return pl.pallas_call(
        functools.partial(_inproj_kernel, tiles_per_seq=tiles_per_seq),
        grid=grid,
        in_specs=[
            pl.BlockSpec((TM_IN, D_MODEL), lambda i, j: (i, 0)),
            pl.BlockSpec((None, 1, D_MODEL), lambda i, j: (layer, 0, 0)),
            pl.BlockSpec((None, D_MODEL, TN_IN), lambda i, j: (layer, 0, j)),
            pl.BlockSpec((None, 1, TN_IN), lambda i, j: (layer, 0, j)),
            pl.BlockSpec((None, D_MODEL, GATE_LANES), lambda i, j: (layer, 0, 0)),
            pl.BlockSpec((None, 1, GATE_LANES), lambda i, j: (layer, 0, 0)),
            pl.BlockSpec((None, CONV_K, TN_IN), lambda i, j: (layer, 0, jnp.minimum(j, n_conv - 1))),
            pl.BlockSpec((None, 1, TN_IN), lambda i, j: (layer, 0, jnp.minimum(j, n_conv - 1))),
            pl.BlockSpec((TM_IN, LANE), lambda i, j: (i % tiles_per_seq, 0)),
            pl.BlockSpec((TM_IN, LANE), lambda i, j: (i % tiles_per_seq, 0)),
        ],
        out_specs=[
            pl.BlockSpec((TM_IN, TN_IN), lambda i, j: (i, j)),
            pl.BlockSpec((TM_IN, GATE_LANES), lambda i, j: (i, 0)),
            pl.BlockSpec((GATE_ROWS, TM_IN), lambda i, j: (0, i)),
        ],
        out_shape=[
            jax.ShapeDtypeStruct((T, Z_COLS), BF16),
            jax.ShapeDtypeStruct((T, GATE_LANES), F32),
            jax.ShapeDtypeStruct((GATE_ROWS, T), F32),
        ],
        scratch_shapes=[
            pltpu.VMEM((TM_IN, D_MODEL), BF16),
            pltpu.VMEM((n_conv, 8, TN_IN), F32),
        ],
        compiler_params=pltpu.CompilerParams(
            dimension_semantics=("arbitrary", "arbitrary"),
            vmem_limit_bytes=VMEM_LIMIT),
        name="inproj",
    )(x2d, norm_g, w_z, b_z, w_g, b_g, conv_w, conv_b, cos, sin)


def _mixer_kernel(q_ref, k_ref, v_ref, o_ref, rq_ref, rk_ref, rv_ref, rg_ref,
                  gc_ref, gr_ref, intra_ref, rcol_ref, mg_ref, rgain_ref,
                  ym_ref, yr_ref,
                  hm_scr, hr_scr, c_scr, r_scr, m_scr, *, cdec):
    L = CHUNK_L
    TS = TS_MIX
    t = pl.program_id(1)

    @pl.when(t == 0)
    def _():
        c_scr[...] = jnp.zeros(c_scr.shape, F32)
        r_scr[...] = jnp.zeros(r_scr.shape, F32)
        m_scr[...] = jnp.zeros(m_scr.shape, F32)

    row_id = lax.broadcasted_iota(jnp.int32, (L, L), 0)
    col_id = lax.broadcasted_iota(jnp.int32, (L, L), 1)
    causal = row_id >= col_id
    tril = causal.astype(BF16)
    triu = (row_id <= col_id).astype(BF16)
    ones_pad = jnp.ones((L, LANE), BF16)
    rcol = rcol_ref[...]

    for c in range(TS // L):
        rows = slice(c * L, (c + 1) * L)
        gcol = gc_ref[rows, :]
        grow = gr_ref[:, rows]
        b_cols = _dot01_left(tril, _log_sigmoid(gcol))
        b_rows = _dot01_right(_log_sigmoid(grow), triu)
        for h in range(HEADS):
            sl = slice(h * M_DK, (h + 1) * M_DK)
            q = q_ref[rows, sl]
            k = k_ref[rows, sl]
            vaug = jnp.concatenate([v_ref[rows, sl], ones_pad], axis=1)
            bcol = b_cols[:, HEADS + h:HEADS + h + 1]
            licol = gcol[:, h:h + 1]
            brow = b_rows[HEADS + h:HEADS + h + 1, :]
            lirow = grow[h:h + 1, :]
            m_prev = m_scr[h][0:1, 0:1]

            rel = jnp.where(causal, lirow - brow, -jnp.inf)
            g_t = jnp.maximum(m_prev, jnp.max(rel, axis=-1, keepdims=True))
            m_t = bcol + g_t
            dmat = jnp.exp(rel - g_t)
            d_inter = jnp.exp(m_prev - g_t)
            s = lax.dot_general(q, k, (((1,), (1,)), ((), ())), preferred_element_type=F32) * dmat
            c_old = c_scr[h]
            num = (jnp.dot(s.astype(BF16), vaug, preferred_element_type=F32)
                   + d_inter * jnp.dot(q, c_old.astype(BF16), preferred_element_type=F32))
            den = num[:, M_DV:M_DV + 1]
            scale = 1.0 / jnp.maximum(jnp.abs(den), jnp.exp(-m_t))
            hm_scr[rows, sl] = num[:, :M_DV] * scale

            g_end = g_t[L - 1:L, :]
            m_new = bcol[L - 1:L, :] + g_end
            w = jnp.exp(licol - bcol - g_end)
            decay = jnp.exp(m_prev - g_end)
            kw = (k.astype(F32) * w).astype(BF16)
            c_scr[h] = decay * c_old + lax.dot_general(kw, vaug, (((0,), (0,)), ((), ())),
                                                       preferred_element_type=F32)
            m_scr[h] = jnp.broadcast_to(m_new, m_scr.shape[1:])

            sl = slice(h * R_DK, (h + 1) * R_DK)
            vs = slice(h * R_DV, (h + 1) * R_DV)
            rq = rq_ref[rows, sl]
            rk = rk_ref[rows, sl]
            rv = rv_ref[rows, vs]
            s = lax.dot_general(rq, rk, (((1,), (1,)), ((), ())), preferred_element_type=F32) * intra_ref[h]
            r_old = r_scr[h]
            hr_scr[rows, vs] = (jnp.dot(s.astype(BF16), rv, preferred_element_type=F32)
                                + jnp.dot(rq, r_old.astype(BF16), preferred_element_type=F32)
                                * rcol[:, h:h + 1])
            rkw = (rk.astype(F32) * rcol[:, HEADS + h:HEADS + h + 1]).astype(BF16)
            r_scr[h] = cdec[h] * r_old + lax.dot_general(rkw, rv, (((0,), (0,)), ((), ())),
                                                         preferred_element_type=F32)

    mg = mg_ref[...]
    rgain = rgain_ref[...]

    def head_norm(x):
        mu = jnp.mean(x, axis=-1, keepdims=True)
        xc = x - mu
        var = jnp.mean(xc * xc, axis=-1, keepdims=True)
        return xc * lax.rsqrt(var + EPS)

    def finish(rows):
        for h in range(HEADS):
            sl = slice(h * M_DV, (h + 1) * M_DV)
            y = o_ref[rows, sl].astype(F32) * (head_norm(hm_scr[rows, sl]) * mg[:, sl])
            ym_ref[rows, sl] = y.astype(BF16)
            vs = slice(h * R_DV, (h + 1) * R_DV)
            y = rg_ref[rows, vs].astype(F32) * (head_norm(hr_scr[rows, vs]) * rgain[:, vs])
            yr_ref[rows, vs] = y.astype(BF16)
    _strips(TS, finish, unroll=4)


def _retention_tables(L):
    gamma = 1.0 - np.exp2(-5.0 - np.arange(HEADS, dtype=np.float64))
    lg = np.log(gamma)
    pos = np.arange(L, dtype=np.float64)
    diff = pos[:, None] - pos[None, :]
    intra = np.where(diff >= 0, np.exp(np.maximum(diff, 0.0) * lg[:, None, None]), 0.0)
    rcol = np.zeros((L, LANE), np.float64)
    rcol[:, :HEADS] = np.exp((pos[:, None] + 1.0) * lg[None, :])
    rcol[:, HEADS:2 * HEADS] = np.exp((L - 1.0 - pos[:, None]) * lg[None, :])
    cdec = tuple(float(v) for v in np.exp(L * lg))
    return jnp.asarray(intra, F32), jnp.asarray(rcol, F32), cdec


def _mixer(layer, batch, seq, z, gcol, grow, m_gain, r_gain):
    L = CHUNK_L
    TS = TS_MIX
    nt = seq // TS
    intra, rcol, cdec = _retention_tables(L)

    def zspec(width, col0):
        blk = col0 // width
        return pl.BlockSpec((TS, width), lambda b, t: (b * nt + t, blk))

    def const(shape):
        return pl.BlockSpec(shape, lambda b, t: tuple(0 for _ in shape))

    def per_layer(width):
        return pl.BlockSpec((None, 1, width), lambda b, t: (layer, 0, 0))

    T = batch * seq
    return pl.pallas_call(
        functools.partial(_mixer_kernel, cdec=cdec),
        grid=(batch, nt),
        in_specs=[
            zspec(M_QK, Z_MQ), zspec(M_QK, Z_MK), zspec(M_V, Z_MV), zspec(M_V, Z_MO),
            zspec(R_QK, Z_RQ), zspec(R_QK, Z_RK), zspec(R_V, Z_RV), zspec(R_V, Z_RG),
            pl.BlockSpec((TS, GATE_LANES), lambda b, t: (b * nt + t, 0)),
            pl.BlockSpec((GATE_ROWS, TS), lambda b, t: (0, b * nt + t)),
            const((HEADS, L, L)), const((L, LANE)),
            per_layer(M_V), per_layer(R_V),
        ],
        out_specs=[
            pl.BlockSpec((TS, M_V), lambda b, t: (b * nt + t, 0)),
            pl.BlockSpec((TS, R_V), lambda b, t: (b * nt + t, 0)),
        ],
        out_shape=[
            jax.ShapeDtypeStruct((T, M_V), BF16),
            jax.ShapeDtypeStruct((T, R_V), BF16),
        ],
        scratch_shapes=[
            pltpu.VMEM((TS, M_V), F32), pltpu.VMEM((TS, R_V), F32),
            pltpu.VMEM((HEADS, M_DK, M_DV + LANE), F32),
            pltpu.VMEM((HEADS, R_DK, R_DV), F32),
            pltpu.VMEM((HEADS, 8, LANE), F32),
        ],
        compiler_params=pltpu.CompilerParams(
            dimension_semantics=("arbitrary", "arbitrary"),
            vmem_limit_bytes=VMEM_LIMIT),
        name="mixer",
    )(z, z, z, z, z, z, z, z, gcol, grow, intra, rcol, m_gain, r_gain)


def _rmsnorm_rows(src_ref, dst_ref, g, n_rows):
    def norm(rows):
        x = src_ref[rows, :]
        ms = jnp.mean(x * x, axis=-1, keepdims=True)
        dst_ref[rows, :] = (x * lax.rsqrt(ms + EPS) * g).astype(dst_ref.dtype)
    _strips(n_rows, norm, unroll=32)


def _post_kernel(ym_ref, yr_ref, gm_ref, gr_ref, x_ref, p_ref,
                 wbm_ref, wbr_ref, wout_ref, g2_ref, w1_ref, b1_ref, w2_ref, b2_ref,
                 g3_ref, wpg_ref, wpe_ref, gf_ref,
                 out_ref, x_scr, h_scr, *, final):
    a = jnp.dot(ym_ref[...], wbm_ref[...], preferred_element_type=F32)
    merged = gm_ref[...].astype(F32) * a
    b = jnp.dot(yr_ref[...], wbr_ref[...], preferred_element_type=F32)
    merged = merged + gr_ref[...].astype(F32) * b
    x_scr[...] = x_ref[...] + jnp.dot(merged.astype(BF16), wout_ref[...], preferred_element_type=F32)

    _rmsnorm_rows(x_scr, h_scr, g2_ref[...], TM_POST)
    h = h_scr[...]
    acc = x_scr[...] + b2_ref[...]
    for c in range(D_FF // D_MODEL):
        cs = slice(c * D_MODEL, (c + 1) * D_MODEL)
        u = jnp.maximum(jnp.dot(h, w1_ref[:, cs], preferred_element_type=F32) + b1_ref[:, cs], 0.0)
        acc = acc + jnp.dot((u * u).astype(BF16), w2_ref[cs, :], preferred_element_type=F32)
    x_scr[...] = acc

    _rmsnorm_rows(x_scr, h_scr, g3_ref[...], TM_POST)
    gate = _sigmoid(jnp.dot(h_scr[...], wpg_ref[...], preferred_element_type=F32))
    pe = jnp.dot(p_ref[...].astype(BF16), wpe_ref[...], preferred_element_type=F32)
    if final:
        x_scr[...] = x_scr[...] + gate * pe
        _rmsnorm_rows(x_scr, out_ref, gf_ref[...], TM_POST)
    else:
        out_ref[...] = x_scr[...] + gate * pe


def _post(layer, final, ym, yr, z, x2d, p3d, w_bm, w_br, w_out, g2, w1, b1, w2, b2, g3, wpg, wpe, gf):
    T = x2d.shape[0]
    tm = TM_POST

    def rows(width, blk=0):
        return pl.BlockSpec((tm, width), lambda i: (i, blk))

    def weight(k, n):
        return pl.BlockSpec((None, k, n), lambda i: (layer, 0, 0), pipeline_mode=pl.Buffered(1))

    def vec(n):
        return pl.BlockSpec((None, 1, n), lambda i: (layer, 0, 0))

    return pl.pallas_call(
        functools.partial(_post_kernel, final=final),
        grid=(T // tm,),
        in_specs=[
            rows(M_V), rows(R_V), rows(D_MODEL, Z_GM // D_MODEL), rows(D_MODEL, Z_GR // D_MODEL),
            rows(D_MODEL),
            pl.BlockSpec((None, tm, PE_DIM), lambda i: (layer, i, 0)),
            weight(M_V, D_MODEL), weight(R_V, D_MODEL), weight(D_MODEL, D_MODEL),
            vec(D_MODEL), weight(D_MODEL, D_FF), vec(D_FF), weight(D_FF, D_MODEL), vec(D_MODEL),
            vec(D_MODEL), weight(D_MODEL, D_MODEL), weight(PE_DIM, D_MODEL),
            pl.BlockSpec((1, D_MODEL), lambda i: (0, 0)),
        ],
        out_specs=rows(D_MODEL),
        out_shape=jax.ShapeDtypeStruct((T, D_MODEL), F32),
        scratch_shapes=[pltpu.VMEM((tm, D_MODEL), F32), pltpu.VMEM((tm, D_MODEL), BF16)],
        compiler_params=pltpu.CompilerParams(
            dimension_semantics=("arbitrary",),
            vmem_limit_bytes=VMEM_LIMIT),
        name="post_final" if final else "post",
    )(ym, yr, z, z, x2d, p3d, w_bm, w_br, w_out, g2, w1, b1, w2, b2, g3, wpg, wpe, gf)


def _reorder_in_proj(w_in, b_in):
    big = [(0, 4096), (4104, 4104 + 6144 + 2048)]
    w_z = jnp.concatenate([w_in[..., a:b].astype(BF16) for a, b in big], axis=-1)
    b_z = jnp.concatenate([b_in[..., a:b] for a, b in big], axis=-1)
    w_gate = w_in[..., 4096:4104]
    b_gate = b_in[..., 4096:4104]
    return w_z, b_z, w_gate, b_gate


def kernel(x, p, norm1_g, w_in, b_in, conv_w, conv_b, m_norm_g, r_norm_g, w_bm, w_br, w_out,
           norm2_g, w_ff1, b_ff1, w_ff2, b_ff2, norm3_g, w_pe_gate, w_pe, final_g):
    B, S, _ = x.shape
    depth = w_in.shape[0]
    T = B * S
    assert S % TS_MIX == 0 and TS_MIX % CHUNK_L == 0 and S % TM_IN == 0 and T % TM_POST == 0

    w_z, b_z, w_gate, b_gate = _reorder_in_proj(w_in, b_in)
    b_z = b_z[:, None, :]
    pad = GATE_LANES - GATE_ROWS
    w_g = jnp.pad(w_gate, ((0, 0), (0, 0), (0, pad))).astype(BF16)
    b_g = jnp.pad(b_gate, ((0, 0), (0, pad)))[:, None, :]

    pos = jnp.arange(S, dtype=F32)
    inv_freq = ROPE_BASE ** (-jnp.arange(0, R_DK, 2, dtype=F32) / R_DK)
    ang = pos[:, None] * inv_freq[None, :]
    cos = jnp.cos(ang)
    sin = jnp.sin(ang)

    bf = lambda w: w.astype(BF16)
    row = lambda v: v[:, None, :]
    w_bm, w_br, w_out, w_ff1, w_ff2, w_pe_gate, w_pe = map(
        bf, (w_bm, w_br, w_out, w_ff1, w_ff2, w_pe_gate, w_pe))
    norm1_g, conv_b, m_norm_g, r_norm_g, norm2_g, b_ff1, b_ff2, norm3_g = map(
        row, (norm1_g, conv_b, m_norm_g, r_norm_g, norm2_g, b_ff1, b_ff2, norm3_g))
    final_g = final_g[None, :]

    x2d = x.reshape(T, D_MODEL)
    p3d = p.reshape(depth, T, PE_DIM)
    for i in range(depth):
        z, gcol, grow = _inproj(i, S, x2d, norm1_g, w_z, b_z, w_g, b_g,
                                conv_w, conv_b, cos, sin)
        ym, yr = _mixer(i, B, S, z, gcol, grow, m_norm_g, r_norm_g)
        x2d = _post(i, i == depth - 1, ym, yr, z, x2d, p3d, w_bm, w_br, w_out,
                    norm2_g, w_ff1, b_ff1, w_ff2, b_ff2, norm3_g, w_pe_gate, w_pe, final_g)
    return x2d.reshape(B, S, D_MODEL)
```

```python
import functools

import numpy as np
import jax
import jax.numpy as jnp
from jax import lax
from jax.experimental import pallas as pl
from jax.experimental.pallas import tpu as pltpu

F32 = jnp.float32
BF16 = jnp.bfloat16

D_MODEL = 1024
PE_DIM = 256
EPS = 1e-6
HEADS = 4
M_DK = 256
M_DV = 256
R_DK = 256
R_DV = 512
M_QK = HEADS * M_DK
M_V = HEADS * M_DV
R_QK = HEADS * R_DK
R_V = HEADS * R_DV
CONV_K = 4
ROPE_BASE = 10000.0
D_FF = 4 * D_MODEL

Z_MQ, Z_MK, Z_MV, Z_MO = 0, 1024, 2048, 3072
Z_RQ, Z_RK, Z_RV, Z_RG = 4096, 5120, 6144, 8192
Z_GM, Z_GR = 10240, 11264
Z_COLS = 12288
GATE_LANES = 128
GATE_ROWS = 8

LANE = 128
VMEM_LIMIT = 60 * 1024 * 1024

TM_IN = 2048
TN_IN = 1024
SUB_ROWS = (1024, 1024)
EPI_COLS = 256
CONV_COLS = 512
CHUNK_L = 256
TS_MIX = 256
TM_POST = 512
STRIP = 16


def _sigmoid(x):
    return 0.5 * jnp.tanh(0.5 * x) + 0.5


def _swish(x):
    u = 0.5 * x
    return u * jnp.tanh(u) + u


def _log_sigmoid(x):
    return jnp.minimum(x, 0.0) - jnp.log1p(jnp.exp(-jnp.abs(x)))


def _split3(x):
    hi = x.astype(BF16)
    r1 = x - hi.astype(F32)
    mid = r1.astype(BF16)
    lo = (r1 - mid.astype(F32)).astype(BF16)
    return hi, mid, lo


def _dot01_left(mat01, x):
    return sum(jnp.dot(mat01, piece, preferred_element_type=F32) for piece in _split3(x))


def _dot01_right(x, mat01):
    return sum(jnp.dot(piece, mat01, preferred_element_type=F32) for piece in _split3(x))


def _strips(n_rows, body, rows_per_strip=STRIP, unroll=1):
    def step(i, carry):
        body(pl.ds(pl.multiple_of(i * rows_per_strip, rows_per_strip), rows_per_strip))
        return carry
    lax.fori_loop(0, n_rows // rows_per_strip, step, 0, unroll=unroll)


def _inproj_kernel(x_ref, g_ref, w_ref, b_ref, wg_ref, bg_ref,
                   cw_ref, cb_ref, cos_ref, sin_ref,
                   z_ref, gc_ref, gr_ref, h_scr, carry_scr, *, tiles_per_seq):
    i = pl.program_id(0)
    j = pl.program_id(1)

    @pl.when(j == 0)
    def _():
        g = g_ref[...]

        def norm(rows):
            x = x_ref[rows, :]
            ms = jnp.mean(x * x, axis=-1, keepdims=True)
            h_scr[rows, :] = (x * lax.rsqrt(ms + EPS) * g).astype(BF16)
        _strips(TM_IN, norm, unroll=32)
        h = h_scr[...]
        gates = jnp.dot(h, wg_ref[...], preferred_element_type=F32) + bg_ref[...]
        gc_ref[...] = gates
        gr_ref[...] = gates.T[:GATE_ROWS, :]

    def run(epilogue, epi_cols=EPI_COLS, add_bias=True):
        assert sum(SUB_ROWS) == TM_IN
        for r0, n in zip(np.cumsum((0,) + SUB_ROWS[:-1]), SUB_ROWS):
            rows = pl.ds(int(r0), n)
            zb = jnp.dot(h_scr[rows, :], w_ref[...], preferred_element_type=F32)
            if add_bias:
                zb = zb + b_ref[...]
            for c0 in range(0, TN_IN, epi_cols):
                cols = slice(c0, c0 + epi_cols)
                z_ref[rows, cols] = epilogue(zb[:, cols], rows, cols).astype(BF16)

    is_conv = j < Z_MV // TN_IN
    is_rope = (j == Z_RQ // TN_IN) | (j == Z_RK // TN_IN)
    is_sigmoid = (j == Z_MO // TN_IN) | (j >= Z_GM // TN_IN)
    is_swish = (j >= Z_RG // TN_IN) & (j < Z_GM // TN_IN)
    is_plain = jnp.logical_not(is_conv | is_rope | is_sigmoid | is_swish)

    @pl.when(is_plain)
    def _():
        z = jnp.dot(h_scr[...], w_ref[...], preferred_element_type=F32) + b_ref[...]
        z_ref[...] = z.astype(BF16)

    @pl.when(is_sigmoid)
    def _():
        run(lambda zc, rows, cols: _sigmoid(zc))

    @pl.when(is_swish)
    def _():
        run(lambda zc, rows, cols: _swish(zc))

    @pl.when(is_rope)
    def _():
        scale = jnp.where(j == Z_RK // TN_IN, R_DK ** -0.5, 1.0)

        def rope(zc, rows, cols):
            c = cos_ref[rows, :] * scale
            s = sin_ref[rows, :] * scale
            x1, x2 = zc[:, :LANE], zc[:, LANE:]
            return jnp.concatenate([x1 * c - x2 * s, x1 * s + x2 * c], axis=1)
        run(rope)

    @pl.when(is_conv)
    def _():
        bias = b_ref[...]
        cw_half = cw_ref[...] * 0.5
        shift = jnp.sum(cw_half, axis=0, keepdims=True) * bias + cb_ref[...] * 0.5

        @pl.when(i % tiles_per_seq == 0)
        def _():
            carry_scr[j] = jnp.broadcast_to(-bias, carry_scr.shape[1:])
        scale = jnp.where(j == Z_MQ // TN_IN, M_DK ** -0.5, 1.0)

        row7 = lax.broadcasted_iota(jnp.int32, (8, CONV_COLS), 0) == 7

        def down_one(above, cur):
            return pltpu.roll(jnp.where(row7, above, cur), 1, axis=0)

        for r0, n in zip(np.cumsum((0,) + SUB_ROWS[:-1]), SUB_ROWS):
            r0 = int(r0)
            d = jnp.dot(h_scr[pl.ds(r0, n), :], w_ref[...], preferred_element_type=F32)
            for c0 in range(0, TN_IN, CONV_COLS):
                cols = slice(c0, c0 + CONV_COLS)
                w = [cw_half[tap:tap + 1, cols] for tap in range(CONV_K)]
                x = carry_scr[j, :, cols]
                above = [x * w[0]]
                for tap in (1, 2):
                    above.append(x * w[tap] + pltpu.roll(above[-1], 1, axis=0))
                done = []
                for p in range(0, n, 8):
                    x = d[p:p + 8, cols]
                    part = [x * w[0]]
                    for tap in (1, 2, 3):
                        part.append(x * w[tap] + down_one(above[tap - 1], part[-1]))
                    above = part[:3]
                    u = part[3] + shift[:, cols]
                    done.append((u * jnp.tanh(u) + u) * scale)
                    if len(done) == 2:
                        z_ref[pl.ds(r0 + p - 8, 16), cols] = jnp.concatenate(done, axis=0).astype(BF16)
                        done = []
                carry_scr[j, :, cols] = x


def _inproj(layer, seq, x2d, norm_g, w_z, b_z, w_g, b_g, conv_w, conv_b, cos, sin):
    T = x2d.shape[0]
    tiles_per_seq = seq // TM_IN
    n_conv = Z_MV // TN_IN
    grid = (T // TM_IN, Z_COLS // TN_IN)
    return pl.pallas_call(
        functools.partial(_inproj_kernel, tiles_per_seq=tiles_per_seq),
        grid=grid,
        in_specs=[
            pl.BlockSpec((TM_IN, D_MODEL), lambda i, j: (i, 0)),
            pl.BlockSpec((None, 1, D_MODEL), lambda i, j: (layer, 0, 0)),
            pl.BlockSpec((None, D_MODEL, TN_IN), lambda i, j: (layer, 0, j)),
            pl.BlockSpec((None, 1, TN_IN), lambda i, j: (layer, 0, j)),
            pl.BlockSpec((None, D_MODEL, GATE_LANES), lambda i, j: (layer, 0, 0)),
            pl.BlockSpec((None, 1, GATE_LANES), lambda i, j: (layer, 0, 0)),
            pl.BlockSpec((None, CONV_K, TN_IN), lambda i, j: (layer, 0, jnp.minimum(j, n_conv - 1))),
            pl.BlockSpec((None, 1, TN_IN), lambda i, j: (layer, 0, jnp.minimum(j, n_conv - 1))),
            pl.BlockSpec((TM_IN, LANE), lambda i, j: (i % tiles_per_seq, 0)),
            pl.BlockSpec((TM_IN, LANE), lambda i, j: (i % tiles_per_seq, 0)),
        ],
        out_specs=[
            pl.BlockSpec((TM_IN, TN_IN), lambda i, j: (i, j)),
            pl.BlockSpec((TM_IN, GATE_LANES), lambda i, j: (i, 0)),
            pl.BlockSpec((GATE_ROWS, TM_IN), lambda i, j: (0, i)),
        ],
        out_shape=[
            jax.ShapeDtypeStruct((T, Z_COLS), BF16),
            jax.ShapeDtypeStruct((T, GATE_LANES), F32),
            jax.ShapeDtypeStruct((GATE_ROWS, T), F32),
        ],
        scratch_shapes=[
            pltpu.VMEM((TM_IN, D_MODEL), BF16),
            pltpu.VMEM((n_conv, 8, TN_IN), F32),
        ],
        compiler_params=pltpu.CompilerParams(
            dimension_semantics=("arbitrary", "arbitrary"),
            vmem_limit_bytes=VMEM_LIMIT),
        name="inproj",
    )(x2d, norm_g, w_z, b_z, w_g, b_g, conv_w, conv_b, cos, sin)


def _mixer_kernel(q_ref, k_ref, v_ref, o_ref, rq_ref, rk_ref, rv_ref, rg_ref,
                  gc_ref, gr_ref, intra_ref, rcol_ref, mg_ref, rgain_ref,
                  ym_ref, yr_ref,
                  hm_scr, hr_scr, c_scr, r_scr, m_scr, *, cdec):
    L = CHUNK_L
    TS = TS_MIX
    t = pl.program_id(1)

    @pl.when(t == 0)
    def _():
        c_scr[...] = jnp.zeros(c_scr.shape, F32)
        r_scr[...] = jnp.zeros(r_scr.shape, F32)
        m_scr[...] = jnp.zeros(m_scr.shape, F32)

    row_id = lax.broadcasted_iota(jnp.int32, (L, L), 0)
    col_id = lax.broadcasted_iota(jnp.int32, (L, L), 1)
    causal = row_id >= col_id
    tril = causal.astype(BF16)
    triu = (row_id <= col_id).astype(BF16)
    ones_pad = jnp.ones((L, LANE), BF16)
    rcol = rcol_ref[...]

    for c in range(TS // L):
        rows = slice(c * L, (c + 1) * L)
        gcol = gc_ref[rows, :]
        grow = gr_ref[:, rows]
        b_cols = _dot01_left(tril, _log_sigmoid(gcol))
        b_rows = _dot01_right(_log_sigmoid(grow), triu)
        for h in range(HEADS):
            sl = slice(h * M_DK, (h + 1) * M_DK)
            q = q_ref[rows, sl]
            k = k_ref[rows, sl]
            vaug = jnp.concatenate([v_ref[rows, sl], ones_pad], axis=1)
            bcol = b_cols[:, HEADS + h:HEADS + h + 1]
            licol = gcol[:, h:h + 1]
            brow = b_rows[HEADS + h:HEADS + h + 1, :]
            lirow = grow[h:h + 1, :]
            m_prev = m_scr[h][0:1, 0:1]

            rel = jnp.where(causal, lirow - brow, -jnp.inf)
            g_t = jnp.maximum(m_prev, jnp.max(rel, axis=-1, keepdims=True))
            m_t = bcol + g_t
            dmat = jnp.exp(rel - g_t)
            d_inter = jnp.exp(m_prev - g_t)
            s = lax.dot_general(q, k, (((1,), (1,)), ((), ())), preferred_element_type=F32) * dmat
            c_old = c_scr[h]
            num = (jnp.dot(s.astype(BF16), vaug, preferred_element_type=F32)
                   + d_inter * jnp.dot(q, c_old.astype(BF16), preferred_element_type=F32))
            den = num[:, M_DV:M_DV + 1]
            scale = 1.0 / jnp.maximum(jnp.abs(den), jnp.exp(-m_t))
            hm_scr[rows, sl] = num[:, :M_DV] * scale

            g_end = g_t[L - 1:L, :]
            m_new = bcol[L - 1:L, :] + g_end
            w = jnp.exp(licol - bcol - g_end)
            decay = jnp.exp(m_prev - g_end)
            kw = (k.astype(F32) * w).astype(BF16)
            c_scr[h] = decay * c_old + lax.dot_general(kw, vaug, (((0,), (0,)), ((), ())),
                                                       preferred_element_type=F32)
            m_scr[h] = jnp.broadcast_to(m_new, m_scr.shape[1:])

            sl = slice(h * R_DK, (h + 1) * R_DK)
            vs = slice(h * R_DV, (h + 1) * R_DV)
            rq = rq_ref[rows, sl]
            rk = rk_ref[rows, sl]
            rv = rv_ref[rows, vs]
            s = lax.dot_general(rq, rk, (((1,), (1,)), ((), ())), preferred_element_type=F32) * intra_ref[h]
            r_old = r_scr[h]
            hr_scr[rows, vs] = (jnp.dot(s.astype(BF16), rv, preferred_element_type=F32)
                                + jnp.dot(rq, r_old.astype(BF16), preferred_element_type=F32)
                                * rcol[:, h:h + 1])
            rkw = (rk.astype(F32) * rcol[:, HEADS + h:HEADS + h + 1]).astype(BF16)
            r_scr[h] = cdec[h] * r_old + lax.dot_general(rkw, rv, (((0,), (0,)), ((), ())),
                                                         preferred_element_type=F32)

    mg = mg_ref[...]
    rgain = rgain_ref[...]

    def head_norm(x):
        mu = jnp.mean(x, axis=-1, keepdims=True)
        xc = x - mu
        var = jnp.mean(xc * xc, axis=-1, keepdims=True)
        return xc * lax.rsqrt(var + EPS)

    def finish(rows):
        for h in range(HEADS):
            sl = slice(h * M_DV, (h + 1) * M_DV)
            y = o_ref[rows, sl].astype(F32) * (head_norm(hm_scr[rows, sl]) * mg[:, sl])
            ym_ref[rows, sl] = y.astype(BF16)
            vs = slice(h * R_DV, (h + 1) * R_DV)
            y = rg_ref[rows, vs].astype(F32) * (head_norm(hr_scr[rows, vs]) * rgain[:, vs])
            yr_ref[rows, vs] = y.astype(BF16)
    _strips(TS, finish, unroll=4)


def _retention_tables(L):
    gamma = 1.0 - np.exp2(-5.0 - np.arange(HEADS, dtype=np.float64))
    lg = np.log(gamma)
    pos = np.arange(L, dtype=np.float64)
    diff = pos[:, None] - pos[None, :]
    intra = np.where(diff >= 0, np.exp(np.maximum(diff, 0.0) * lg[:, None, None]), 0.0)
    rcol = np.zeros((L, LANE), np.float64)
    rcol[:, :HEADS] = np.exp((pos[:, None] + 1.0) * lg[None, :])
    rcol[:, HEADS:2 * HEADS] = np.exp((L - 1.0 - pos[:, None]) * lg[None, :])
    cdec = tuple(float(v) for v in np.exp(L * lg))
    return jnp.asarray(intra, F32), jnp.asarray(rcol, F32), cdec


def _mixer(layer, batch, seq, z, gcol, grow, m_gain, r_gain):
    L = CHUNK_L
    TS = TS_MIX
    nt = seq // TS
    intra, rcol, cdec = _retention_tables(L)

    def zspec(width, col0):
        blk = col0 // width
        return pl.BlockSpec((TS, width), lambda b, t: (b * nt + t, blk))

    def const(shape):
        return pl.BlockSpec(shape, lambda b, t: tuple(0 for _ in shape))

    def per_layer(width):
        return pl.BlockSpec((None, 1, width), lambda b, t: (layer, 0, 0))

    T = batch * seq
    return pl.pallas_call(
        functools.partial(_mixer_kernel, cdec=cdec),
        grid=(batch, nt),
        in_specs=[
            zspec(M_QK, Z_MQ), zspec(M_QK, Z_MK), zspec(M_V, Z_MV), zspec(M_V, Z_MO),
            zspec(R_QK, Z_RQ), zspec(R_QK, Z_RK), zspec(R_V, Z_RV), zspec(R_V, Z_RG),
            pl.BlockSpec((TS, GATE_LANES), lambda b, t: (b * nt + t, 0)),
            pl.BlockSpec((GATE_ROWS, TS), lambda b, t: (0, b * nt + t)),
            const((HEADS, L, L)), const((L, LANE)),
            per_layer(M_V), per_layer(R_V),
        ],
        out_specs=[
            pl.BlockSpec((TS, M_V), lambda b, t: (b * nt + t, 0)),
            pl.BlockSpec((TS, R_V), lambda b, t: (b * nt + t, 0)),
        ],
        out_shape=[
            jax.ShapeDtypeStruct((T, M_V), BF16),
            jax.ShapeDtypeStruct((T, R_V), BF16),
        ],
        scratch_shapes=[
            pltpu.VMEM((TS, M_V), F32), pltpu.VMEM((TS, R_V), F32),
            pltpu.VMEM((HEADS, M_DK, M_DV + LANE), F32),
            pltpu.VMEM((HEADS, R_DK, R_DV), F32),
            pltpu.VMEM((HEADS, 8, LANE), F32),
        ],
        compiler_params=pltpu.CompilerParams(
            dimension_semantics=("arbitrary", "arbitrary"),
            vmem_limit_bytes=VMEM_LIMIT),
        name="mixer",
    )(z, z, z, z, z, z, z, z, gcol, grow, intra, rcol, m_gain, r_gain)


def _rmsnorm_rows(src_ref, dst_ref, g, n_rows):
    def norm(rows):
        x = src_ref[rows, :]
        ms = jnp.mean(x * x, axis=-1, keepdims=True)
        dst_ref[rows, :] = (x * lax.rsqrt(ms + EPS) * g).astype(dst_ref.dtype)
    _strips(n_rows, norm, unroll=32)


def _post_kernel(ym_ref, yr_ref, gm_ref, gr_ref, x_ref, p_ref,
                 wbm_ref, wbr_ref, wout_ref, g2_ref, w1_ref, b1_ref, w2_ref, b2_ref,
                 g3_ref, wpg_ref, wpe_ref, gf_ref,
                 out_ref, x_scr, h_scr, *, final):
    a = jnp.dot(ym_ref[...], wbm_ref[...], preferred_element_type=F32)
    merged = gm_ref[...].astype(F32) * a
    b = jnp.dot(yr_ref[...], wbr_ref[...], preferred_element_type=F32)
    merged = merged + gr_ref[...].astype(F32) * b
    x_scr[...] = x_ref[...] + jnp.dot(merged.astype(BF16), wout_ref[...], preferred_element_type=F32)

    _rmsnorm_rows(x_scr, h_scr, g2_ref[...], TM_POST)
    h = h_scr[...]
    acc = x_scr[...] + b2_ref[...]
    for c in range(D_FF // D_MODEL):
        cs = slice(c * D_MODEL, (c + 1) * D_MODEL)
        u = jnp.maximum(jnp.dot(h, w1_ref[:, cs], preferred_element_type=F32) + b1_ref[:, cs], 0.0)
        acc = acc + jnp.dot((u * u).astype(BF16), w2_ref[cs, :], preferred_element_type=F32)
    x_scr[...] = acc

    _rmsnorm_rows(x_scr, h_scr, g3_ref[...], TM_POST)
    gate = _sigmoid(jnp.dot(h_scr[...], wpg_ref[...], preferred_element_type=F32))
    pe = jnp.dot(p_ref[...].astype(BF16), wpe_ref[...], preferred_element_type=F32)
    if final:
        x_scr[...] = x_scr[...] + gate * pe
        _rmsnorm_rows(x_scr, out_ref, gf_ref[...], TM_POST)
    else:
        out_ref[...] = x_scr[...] + gate * pe


def _post(layer, final, ym, yr, z, x2d, p3d, w_bm, w_br, w_out, g2, w1, b1, w2, b2, g3, wpg, wpe, gf):
    T = x2d.shape[0]
    tm = TM_POST

    def rows(width, blk=0):
        return pl.BlockSpec((tm, width), lambda i: (i, blk))

    def weight(k, n):
        return pl.BlockSpec((None, k, n), lambda i: (layer, 0, 0), pipeline_mode=pl.Buffered(1))

    def vec(n):
        return pl.BlockSpec((None, 1, n), lambda i: (layer, 0, 0))

    return pl.pallas_call(
        functools.partial(_post_kernel, final=final),
        grid=(T // tm,),
        in_specs=[
            rows(M_V), rows(R_V), rows(D_MODEL, Z_GM // D_MODEL), rows(D_MODEL, Z_GR // D_MODEL),
            rows(D_MODEL),
            pl.BlockSpec((None, tm, PE_DIM), lambda i: (layer, i, 0)),
            weight(M_V, D_MODEL), weight(R_V, D_MODEL), weight(D_MODEL, D_MODEL),
            vec(D_MODEL), weight(D_MODEL, D_FF), vec(D_FF), weight(D_FF, D_MODEL), vec(D_MODEL),
            vec(D_MODEL), weight(D_MODEL, D_MODEL), weight(PE_DIM, D_MODEL),
            pl.BlockSpec((1, D_MODEL), lambda i: (0, 0)),
        ],
        out_specs=rows(D_MODEL),
        out_shape=jax.ShapeDtypeStruct((T, D_MODEL), F32),
        scratch_shapes=[pltpu.VMEM((tm, D_MODEL), F32), pltpu.VMEM((tm, D_MODEL), BF16)],
        compiler_params=pltpu.CompilerParams(
            dimension_semantics=("arbitrary",),
            vmem_limit_bytes=VMEM_LIMIT),
        name="post_final" if final else "post",
    )(ym, yr, z, z, x2d, p3d, w_bm, w_br, w_out, g2, w1, b1, w2, b2, g3, wpg, wpe, gf)


def _reorder_in_proj(w_in, b_in):
    big = [(0, 4096), (4104, 4104 + 6144 + 2048)]
    w_z = jnp.concatenate([w_in[..., a:b].astype(BF16) for a, b in big], axis=-1)
    b_z = jnp.concatenate([b_in[..., a:b] for a, b in big], axis=-1)
    w_gate = w_in[..., 4096:4104]
    b_gate = b_in[..., 4096:4104]
    return w_z, b_z, w_gate, b_gate


def kernel(x, p, norm1_g, w_in, b_in, conv_w, conv_b, m_norm_g, r_norm_g, w_bm, w_br, w_out,
           norm2_g, w_ff1, b_ff1, w_ff2, b_ff2, norm3_g, w_pe_gate, w_pe, final_g):
    B, S, _ = x.shape
    depth = w_in.shape[0]
    T = B * S
    assert S % TS_MIX == 0 and TS_MIX % CHUNK_L == 0 and S % TM_IN == 0 and T % TM_POST == 0

    w_z, b_z, w_gate, b_gate = _reorder_in_proj(w_in, b_in)
    b_z = b_z[:, None, :]
    pad = GATE_LANES - GATE_ROWS
    w_g = jnp.pad(w_gate, ((0, 0), (0, 0), (0, pad))).astype(BF16)
    b_g = jnp.pad(b_gate, ((0, 0), (0, pad)))[:, None, :]

    pos = jnp.arange(S, dtype=F32)
    inv_freq = ROPE_BASE ** (-jnp.arange(0, R_DK, 2, dtype=F32) / R_DK)
    ang = pos[:, None] * inv_freq[None, :]
    cos = jnp.cos(ang)
    sin = jnp.sin(ang)

    bf = lambda w: w.astype(BF16)
    row = lambda v: v[:, None, :]
    w_bm, w_br, w_out, w_ff1, w_ff2, w_pe_gate, w_pe = map(
        bf, (w_bm, w_br, w_out, w_ff1, w_ff2, w_pe_gate, w_pe))
    norm1_g, conv_b, m_norm_g, r_norm_g, norm2_g, b_ff1, b_ff2, norm3_g = map(
        row, (norm1_g, conv_b, m_norm_g, r_norm_g, norm2_g, b_ff1, b_ff2, norm3_g))
    final_g = final_g[None, :]

    x2d = x.reshape(T, D_MODEL)
    p3d = p.reshape(depth, T, PE_DIM)
    for i in range(depth):
        z, gcol, grow = _inproj(i, S, x2d, norm1_g, w_z, b_z, w_g, b_g,
                                conv_w, conv_b, cos, sin)
        ym, yr = _mixer(i, B, S, z, gcol, grow, m_norm_g, r_norm_g)
        x2d = _post(i, i == depth - 1, ym, yr, z, x2d, p3d, w_bm, w_br, w_out,
                    norm2_g, w_ff1, b_ff1, w_ff2, b_ff2, norm3_g, w_pe_gate, w_pe, final_g)
    return x2d.reshape(B, S, D_MODEL)
```

```python
import functools

import numpy as np
import jax
import jax.numpy as jnp
from jax import lax
from jax.experimental import pallas as pl
from jax.experimental.pallas import tpu as pltpu

F32 = jnp.float32
BF16 = jnp.bfloat16

D_MODEL = 1024
PE_DIM = 256
EPS = 1e-6
HEADS = 4
M_DK = 256
M_DV = 256
R_DK = 256
R_DV = 512
M_QK = HEADS * M_DK
M_V = HEADS * M_DV
R_QK = HEADS * R_DK
R_V = HEADS * R_DV
CONV_K = 4
ROPE_BASE = 10000.0
D_FF = 4 * D_MODEL

Z_MQ, Z_MK, Z_MV, Z_MO = 0, 1024, 2048, 3072
Z_RQ, Z_RK, Z_RV, Z_RG = 4096, 5120, 6144, 8192
Z_GM, Z_GR = 10240, 11264
Z_COLS = 12288
GATE_LANES = 128
GATE_ROWS = 8

LANE = 128
VMEM_LIMIT = 60 * 1024 * 1024

TM_IN = 2048
TN_IN = 1024
SUB_ROWS = (1024, 1024)
EPI_COLS = 256
CONV_COLS = 512
CHUNK_L = 256
TS_MIX = 256
TM_POST = 512
TM_MERGE = 1024
STRIP = 16


def _sigmoid(x):
    return 0.5 * jnp.tanh(0.5 * x) + 0.5


def _swish(x):
    u = 0.5 * x
    return u * jnp.tanh(u) + u


def _log_sigmoid(x):
    return jnp.minimum(x, 0.0) - jnp.log1p(jnp.exp(-jnp.abs(x)))


def _split3(x):
    hi = x.astype(BF16)
    r1 = x - hi.astype(F32)
    mid = r1.astype(BF16)
    lo = (r1 - mid.astype(F32)).astype(BF16)
    return hi, mid, lo


def _dot01_left(mat01, x):
    return sum(jnp.dot(mat01, piece, preferred_element_type=F32) for piece in _split3(x))


def _dot01_right(x, mat01):
    return sum(jnp.dot(piece, mat01, preferred_element_type=F32) for piece in _split3(x))


def _strips(n_rows, body, rows_per_strip=STRIP, unroll=1):
    def step(i, carry):
        body(pl.ds(pl.multiple_of(i * rows_per_strip, rows_per_strip), rows_per_strip))
        return carry
    lax.fori_loop(0, n_rows // rows_per_strip, step, 0, unroll=unroll)


def _inproj_kernel(x_ref, g_ref, w_ref, b_ref, wg_ref, bg_ref,
                   cw_ref, cb_ref, cos_ref, sin_ref,
                   z_ref, gc_ref, gr_ref, h_scr, carry_scr, *, tiles_per_seq):
    i = pl.program_id(0)
    j = pl.program_id(1)

    @pl.when(j == 0)
    def _():
        g = g_ref[...]

        def norm(rows):
            x = x_ref[rows, :]
            ms = jnp.mean(x * x, axis=-1, keepdims=True)
            h_scr[rows, :] = (x * lax.rsqrt(ms + EPS) * g).astype(BF16)
        _strips(TM_IN, norm, unroll=32)
        h = h_scr[...]
        gates = jnp.dot(h, wg_ref[...], preferred_element_type=F32) + bg_ref[...]
        gc_ref[...] = gates
        gr_ref[...] = gates.T[:GATE_ROWS, :]

    def run(epilogue, epi_cols=EPI_COLS, add_bias=True):
        assert sum(SUB_ROWS) == TM_IN
        for r0, n in zip(np.cumsum((0,) + SUB_ROWS[:-1]), SUB_ROWS):
            rows = pl.ds(int(r0), n)
            zb = jnp.dot(h_scr[rows, :], w_ref[...], preferred_element_type=F32)
            if add_bias:
                zb = zb + b_ref[...]
            for c0 in range(0, TN_IN, epi_cols):
                cols = slice(c0, c0 + epi_cols)
                z_ref[rows, cols] = epilogue(zb[:, cols], rows, cols).astype(BF16)

    is_conv = j < Z_MV // TN_IN
    is_rope = (j == Z_RQ // TN_IN) | (j == Z_RK // TN_IN)
    is_sigmoid = (j == Z_MO // TN_IN) | (j >= Z_GM // TN_IN)
    is_swish = (j >= Z_RG // TN_IN) & (j < Z_GM // TN_IN)
    is_plain = jnp.logical_not(is_conv | is_rope | is_sigmoid | is_swish)

    @pl.when(is_plain)
    def _():
        z = jnp.dot(h_scr[...], w_ref[...], preferred_element_type=F32) + b_ref[...]
        z_ref[...] = z.astype(BF16)

    @pl.when(is_sigmoid)
    def _():
        run(lambda zc, rows, cols: _sigmoid(zc))

    @pl.when(is_swish)
    def _():
        run(lambda zc, rows, cols: _swish(zc))

    @pl.when(is_rope)
    def _():
        scale = jnp.where(j == Z_RK // TN_IN, R_DK ** -0.5, 1.0)

        def rope(zc, rows, cols):
            c = cos_ref[rows, :] * scale
            s = sin_ref[rows, :] * scale
            x1, x2 = zc[:, :LANE], zc[:, LANE:]
            return jnp.concatenate([x1 * c - x2 * s, x1 * s + x2 * c], axis=1)
        run(rope)

    @pl.when(is_conv)
    def _():
        bias = b_ref[...]
        cw_half = cw_ref[...] * 0.5
        shift = jnp.sum(cw_half, axis=0, keepdims=True) * bias + cb_ref[...] * 0.5

        @pl.when(i % tiles_per_seq == 0)
        def _():
            carry_scr[j] = jnp.broadcast_to(-bias, carry_scr.shape[1:])
        scale = jnp.where(j == Z_MQ // TN_IN, M_DK ** -0.5, 1.0)

        row7 = lax.broadcasted_iota(jnp.int32, (8, CONV_COLS), 0) == 7

        def down_one(above, cur):
            return pltpu.roll(jnp.where(row7, above, cur), 1, axis=0)

        for r0, n in zip(np.cumsum((0,) + SUB_ROWS[:-1]), SUB_ROWS):
            r0 = int(r0)
            d = jnp.dot(h_scr[pl.ds(r0, n), :], w_ref[...], preferred_element_type=F32)
            for c0 in range(0, TN_IN, CONV_COLS):
                cols = slice(c0, c0 + CONV_COLS)
                w = [cw_half[tap:tap + 1, cols] for tap in range(CONV_K)]
                x = carry_scr[j, :, cols]
                above = [x * w[0]]
                for tap in (1, 2):
                    above.append(x * w[tap] + pltpu.roll(above[-1], 1, axis=0))
                done = []
                for p in range(0, n, 8):
                    x = d[p:p + 8, cols]
                    part = [x * w[0]]
                    for tap in (1, 2, 3):
                        part.append(x * w[tap] + down_one(above[tap - 1], part[-1]))
                    above = part[:3]
                    u = part[3] + shift[:, cols]
                    done.append((u * jnp.tanh(u) + u) * scale)
                    if len(done) == 2:
                        z_ref[pl.ds(r0 + p - 8, 16), cols] = jnp.concatenate(done, axis=0).astype(BF16)
                        done = []
                carry_scr[j, :, cols] = x


def _inproj(layer, seq, x2d, norm_g, w_z, b_z, w_g, b_g, conv_w, conv_b, cos, sin):
    T = x2d.shape[0]
    tiles_per_seq = seq // TM_IN
    n_conv = Z_MV // TN_IN
    grid = (T // TM_IN, Z_COLS // TN_IN)
    return pl.pallas_call(
        functools.partial(_inproj_kernel, tiles_per_seq=tiles_per_seq),
        grid=grid,
        in_specs=[
            pl.BlockSpec((TM_IN, D_MODEL), lambda i, j: (i, 0)),
            pl.BlockSpec((None, 1, D_MODEL), lambda i, j: (layer, 0, 0)),
            pl.BlockSpec((None, D_MODEL, TN_IN), lambda i, j: (layer, 0, j)),
            pl.BlockSpec((None, 1, TN_IN), lambda i, j: (layer, 0, j)),
            pl.BlockSpec((None, D_MODEL, GATE_LANES), lambda i, j: (layer, 0, 0)),
            pl.BlockSpec((None, 1, GATE_LANES), lambda i, j: (layer, 0, 0)),
            pl.BlockSpec((None, CONV_K, TN_IN), lambda i, j: (layer, 0, jnp.minimum(j, n_conv - 1))),
            pl.BlockSpec((None, 1, TN_IN), lambda i, j: (layer, 0, jnp.minimum(j, n_conv - 1))),
            pl.BlockSpec((TM_IN, LANE), lambda i, j: (i % tiles_per_seq, 0)),
            pl.BlockSpec((TM_IN, LANE), lambda i, j: (i % tiles_per_seq, 0)),
        ],
        out_specs=[
            pl.BlockSpec((TM_IN, TN_IN), lambda i, j: (i, j)),
            pl.BlockSpec((TM_IN, GATE_LANES), lambda i, j: (i, 0)),
            pl.BlockSpec((GATE_ROWS, TM_IN), lambda i, j: (0, i)),
        ],
        out_shape=[
            jax.ShapeDtypeStruct((T, Z_COLS), BF16),
            jax.ShapeDtypeStruct((T, GATE_LANES), F32),
            jax.ShapeDtypeStruct((GATE_ROWS, T), F32),
        ],
        scratch_shapes=[
            pltpu.VMEM((TM_IN, D_MODEL), BF16),
            pltpu.VMEM((n_conv, 8, TN_IN), F32),
        ],
        compiler_params=pltpu.CompilerParams(
            dimension_semantics=("arbitrary", "arbitrary"),
            vmem_limit_bytes=VMEM_LIMIT),
        name="inproj",
    )(x2d, norm_g, w_z, b_z, w_g, b_g, conv_w, conv_b, cos, sin)


def _mixer_kernel(q_ref, k_ref, v_ref, o_ref, rq_ref, rk_ref, rv_ref, rg_ref,
                  gc_ref, gr_ref, intra_ref, rcol_ref, mg_ref, rgain_ref,
                  ym_ref, yr_ref,
                  hm_scr, hr_scr, c_scr, r_scr, m_scr, *, cdec):
    L = CHUNK_L
    TS = TS_MIX
    t = pl.program_id(1)

    @pl.when(t == 0)
    def _():
        c_scr[...] = jnp.zeros(c_scr.shape, F32)
        r_scr[...] = jnp.zeros(r_scr.shape, F32)
        m_scr[...] = jnp.zeros(m_scr.shape, F32)

    row_id = lax.broadcasted_iota(jnp.int32, (L, L), 0)
    col_id = lax.broadcasted_iota(jnp.int32, (L, L), 1)
    causal = row_id >= col_id
    tril = causal.astype(BF16)
    triu = (row_id <= col_id).astype(BF16)
    ones_pad = jnp.ones((L, LANE), BF16)
    rcol = rcol_ref[...]

    for c in range(TS // L):
        rows = slice(c * L, (c + 1) * L)
        gcol = gc_ref[rows, :]
        grow = gr_ref[:, rows]
        b_cols = _dot01_left(tril, _log_sigmoid(gcol))
        b_rows = _dot01_right(_log_sigmoid(grow), triu)
        for h in range(HEADS):
            sl = slice(h * M_DK, (h + 1) * M_DK)
            q = q_ref[rows, sl]
            k = k_ref[rows, sl]
            vaug = jnp.concatenate([v_ref[rows, sl], ones_pad], axis=1)
            bcol = b_cols[:, HEADS + h:HEADS + h + 1]
            licol = gcol[:, h:h + 1]
            brow = b_rows[HEADS + h:HEADS + h + 1, :]
            lirow = grow[h:h + 1, :]
            m_prev = m_scr[h][0:1, 0:1]

            rel = jnp.where(causal, lirow - brow, -jnp.inf)
            g_t = jnp.maximum(m_prev, jnp.max(rel, axis=-1, keepdims=True))
            m_t = bcol + g_t
            dmat = jnp.exp(rel - g_t)
            d_inter = jnp.exp(m_prev - g_t)
            s = lax.dot_general(q, k, (((1,), (1,)), ((), ())), preferred_element_type=F32) * dmat
            c_old = c_scr[h]
            num = (jnp.dot(s.astype(BF16), vaug, preferred_element_type=F32)
                   + d_inter * jnp.dot(q, c_old.astype(BF16), preferred_element_type=F32))
            den = num[:, M_DV:M_DV + 1]
            scale = 1.0 / jnp.maximum(jnp.abs(den), jnp.exp(-m_t))
            hm_scr[rows, sl] = num[:, :M_DV] * scale

            g_end = g_t[L - 1:L, :]
            m_new = bcol[L - 1:L, :] + g_end
            w = jnp.exp(licol - bcol - g_end)
            decay = jnp.exp(m_prev - g_end)
            kw = (k.astype(F32) * w).astype(BF16)
            c_scr[h] = decay * c_old + lax.dot_general(kw, vaug, (((0,), (0,)), ((), ())),
                                                       preferred_element_type=F32)
            m_scr[h] = jnp.broadcast_to(m_new, m_scr.shape[1:])

            sl = slice(h * R_DK, (h + 1) * R_DK)
            vs = slice(h * R_DV, (h + 1) * R_DV)
            rq = rq_ref[rows, sl]
            rk = rk_ref[rows, sl]
            rv = rv_ref[rows, vs]
            s = lax.dot_general(rq, rk, (((1,), (1,)), ((), ())), preferred_element_type=F32) * intra_ref[h]
            r_old = r_scr[h]
            hr_scr[rows, vs] = (jnp.dot(s.astype(BF16), rv, preferred_element_type=F32)
                                + jnp.dot(rq, r_old.astype(BF16), preferred_element_type=F32)
                                * rcol[:, h:h + 1])
            rkw = (rk.astype(F32) * rcol[:, HEADS + h:HEADS + h + 1]).astype(BF16)
            r_scr[h] = cdec[h] * r_old + lax.dot_general(rkw, rv, (((0,), (0,)), ((), ())),
                                                         preferred_element_type=F32)

    mg = mg_ref[...]
    rgain = rgain_ref[...]

    def head_norm(x):
        mu = jnp.mean(x, axis=-1, keepdims=True)
        xc = x - mu
        var = jnp.mean(xc * xc, axis=-1, keepdims=True)
        return xc * lax.rsqrt(var + EPS)

    def finish(rows):
        for h in range(HEADS):
            sl = slice(h * M_DV, (h + 1) * M_DV)
            y = o_ref[rows, sl].astype(F32) * (head_norm(hm_scr[rows, sl]) * mg[:, sl])
            ym_ref[rows, sl] = y.astype(BF16)
            vs = slice(h * R_DV, (h + 1) * R_DV)
            y = rg_ref[rows, vs].astype(F32) * (head_norm(hr_scr[rows, vs]) * rgain[:, vs])
            yr_ref[rows, vs] = y.astype(BF16)
    _strips(TS, finish, unroll=4)


def _retention_tables(L):
    gamma = 1.0 - np.exp2(-5.0 - np.arange(HEADS, dtype=np.float64))
    lg = np.log(gamma)
    pos = np.arange(L, dtype=np.float64)
    diff = pos[:, None] - pos[None, :]
    intra = np.where(diff >= 0, np.exp(np.maximum(diff, 0.0) * lg[:, None, None]), 0.0)
    rcol = np.zeros((L, LANE), np.float64)
    rcol[:, :HEADS] = np.exp((pos[:, None] + 1.0) * lg[None, :])
    rcol[:, HEADS:2 * HEADS] = np.exp((L - 1.0 - pos[:, None]) * lg[None, :])
    cdec = tuple(float(v) for v in np.exp(L * lg))
    return jnp.asarray(intra, F32), jnp.asarray(rcol, F32), cdec


def _mixer(layer, batch, seq, z, gcol, grow, m_gain, r_gain):
    L = CHUNK_L
    TS = TS_MIX
    nt = seq // TS
    intra, rcol, cdec = _retention_tables(L)

    def zspec(width, col0):
        blk = col0 // width
        return pl.BlockSpec((TS, width), lambda b, t: (b * nt + t, blk))

    def const(shape):
        return pl.BlockSpec(shape, lambda b, t: tuple(0 for _ in shape))

    def per_layer(width):
        return pl.BlockSpec((None, 1, width), lambda b, t: (layer, 0, 0))

    T = batch * seq
    return pl.pallas_call(
        functools.partial(_mixer_kernel, cdec=cdec),
        grid=(batch, nt),
        in_specs=[
            zspec(M_QK, Z_MQ), zspec(M_QK, Z_MK), zspec(M_V, Z_MV), zspec(M_V, Z_MO),
            zspec(R_QK, Z_RQ), zspec(R_QK, Z_RK), zspec(R_V, Z_RV), zspec(R_V, Z_RG),
            pl.BlockSpec((TS, GATE_LANES), lambda b, t: (b * nt + t, 0)),
            pl.BlockSpec((GATE_ROWS, TS), lambda b, t: (0, b * nt + t)),
            const((HEADS, L, L)), const((L, LANE)),
            per_layer(M_V), per_layer(R_V),
        ],
        out_specs=[
            pl.BlockSpec((TS, M_V), lambda b, t: (b * nt + t, 0)),
            pl.BlockSpec((TS, R_V), lambda b, t: (b * nt + t, 0)),
        ],
        out_shape=[
            jax.ShapeDtypeStruct((T, M_V), BF16),
            jax.ShapeDtypeStruct((T, R_V), BF16),
        ],
        scratch_shapes=[
            pltpu.VMEM((TS, M_V), F32), pltpu.VMEM((TS, R_V), F32),
            pltpu.VMEM((HEADS, M_DK, M_DV + LANE), F32),
            pltpu.VMEM((HEADS, R_DK, R_DV), F32),
            pltpu.VMEM((HEADS, 8, LANE), F32),
        ],
        compiler_params=pltpu.CompilerParams(
            dimension_semantics=("arbitrary", "arbitrary"),
            vmem_limit_bytes=VMEM_LIMIT),
        name="mixer",
    )(z, z, z, z, z, z, z, z, gcol, grow, intra, rcol, m_gain, r_gain)


def _rmsnorm_rows(src_ref, dst_ref, g, n_rows):
    def norm(rows):
        x = src_ref[rows, :]
        ms = jnp.mean(x * x, axis=-1, keepdims=True)
        dst_ref[rows, :] = (x * lax.rsqrt(ms + EPS) * g).astype(dst_ref.dtype)
    _strips(n_rows, norm, unroll=32)


def _merge_kernel(ym_ref, yr_ref, gm_ref, gr_ref, x_ref, wbm_ref, wbr_ref, wout_ref, out_ref):
    a = jnp.dot(ym_ref[...], wbm_ref[...], preferred_element_type=F32)
    merged = gm_ref[...].astype(F32) * a
    b = jnp.dot(yr_ref[...], wbr_ref[...], preferred_element_type=F32)
    merged = merged + gr_ref[...].astype(F32) * b
    out_ref[...] = x_ref[...] + jnp.dot(merged.astype(BF16), wout_ref[...], preferred_element_type=F32)


def _merge(layer, ym, yr, z, x2d, w_bm, w_br, w_out):
    T = x2d.shape[0]
    tm = TM_MERGE

    def rows(width, blk=0):
        return pl.BlockSpec((tm, width), lambda i: (i, blk))

    def weight(k, n):
        return pl.BlockSpec((None, k, n), lambda i: (layer, 0, 0), pipeline_mode=pl.Buffered(1))

    return pl.pallas_call(
        _merge_kernel,
        grid=(T // tm,),
        in_specs=[
            rows(M_V), rows(R_V), rows(D_MODEL, Z_GM // D_MODEL), rows(D_MODEL, Z_GR // D_MODEL),
            rows(D_MODEL),
            weight(M_V, D_MODEL), weight(R_V, D_MODEL), weight(D_MODEL, D_MODEL),
        ],
        out_specs=rows(D_MODEL),
        out_shape=jax.ShapeDtypeStruct((T, D_MODEL), F32),
        compiler_params=pltpu.CompilerParams(
            dimension_semantics=("arbitrary",),
            vmem_limit_bytes=VMEM_LIMIT),
        name="merge",
    )(ym, yr, z, z, x2d, w_bm, w_br, w_out)


def _post_kernel(x_ref, p_ref, g2_ref, w1_ref, b1_ref, w2_ref, b2_ref,
                 g3_ref, wpg_ref, wpe_ref, gf_ref,
                 out_ref, x_scr, h_scr, *, final):
    _rmsnorm_rows(x_ref, h_scr, g2_ref[...], TM_POST)
    h = h_scr[...]
    acc = x_ref[...] + b2_ref[...]
    for c in range(D_FF // D_MODEL):
        cs = slice(c * D_MODEL, (c + 1) * D_MODEL)
        u = jnp.maximum(jnp.dot(h, w1_ref[:, cs], preferred_element_type=F32) + b1_ref[:, cs], 0.0)
        acc = acc + jnp.dot((u * u).astype(BF16), w2_ref[cs, :], preferred_element_type=F32)
    x_scr[...] = acc

    _rmsnorm_rows(x_scr, h_scr, g3_ref[...], TM_POST)
    gate = _sigmoid(jnp.dot(h_scr[...], wpg_ref[...], preferred_element_type=F32))
    pe = jnp.dot(p_ref[...].astype(BF16), wpe_ref[...], preferred_element_type=F32)
    if final:
        x_scr[...] = x_scr[...] + gate * pe
        _rmsnorm_rows(x_scr, out_ref, gf_ref[...], TM_POST)
    else:
        out_ref[...] = x_scr[...] + gate * pe


def _post(layer, final, x2d, p3d, g2, w1, b1, w2, b2, g3, wpg, wpe, gf):
    T = x2d.shape[0]
    tm = TM_POST

    def rows(width, blk=0):
        return pl.BlockSpec((tm, width), lambda i: (i, blk))

    def weight(k, n):
        return pl.BlockSpec((None, k, n), lambda i: (layer, 0, 0), pipeline_mode=pl.Buffered(1))

    def vec(n):
        return pl.BlockSpec((None, 1, n), lambda i: (layer, 0, 0))

    return pl.pallas_call(
        functools.partial(_post_kernel, final=final),
        grid=(T // tm,),
        in_specs=[
            rows(D_MODEL),
            pl.BlockSpec((None, tm, PE_DIM), lambda i: (layer, i, 0)),
            vec(D_MODEL), weight(D_MODEL, D_FF), vec(D_FF), weight(D_FF, D_MODEL), vec(D_MODEL),
            vec(D_MODEL), weight(D_MODEL, D_MODEL), weight(PE_DIM, D_MODEL),
            pl.BlockSpec((1, D_MODEL), lambda i: (0, 0)),
        ],
        out_specs=rows(D_MODEL),
        out_shape=jax.ShapeDtypeStruct((T, D_MODEL), F32),
        scratch_shapes=[pltpu.VMEM((tm, D_MODEL), F32), pltpu.VMEM((tm, D_MODEL), BF16)],
        compiler_params=pltpu.CompilerParams(
            dimension_semantics=("arbitrary",),
            vmem_limit_bytes=VMEM_LIMIT),
        name="post_final" if final else "post",
    )(x2d, p3d, g2, w1, b1, w2, b2, g3, wpg, wpe, gf)


def _reorder_in_proj(w_in, b_in):
    big = [(0, 4096), (4104, 4104 + 6144 + 2048)]
    w_z = jnp.concatenate([w_in[..., a:b].astype(BF16) for a, b in big], axis=-1)
    b_z = jnp.concatenate([b_in[..., a:b] for a, b in big], axis=-1)
    w_gate = w_in[..., 4096:4104]
    b_gate = b_in[..., 4096:4104]
    return w_z, b_z, w_gate, b_gate


def kernel(x, p, norm1_g, w_in, b_in, conv_w, conv_b, m_norm_g, r_norm_g, w_bm, w_br, w_out,
           norm2_g, w_ff1, b_ff1, w_ff2, b_ff2, norm3_g, w_pe_gate, w_pe, final_g):
    B, S, _ = x.shape
    depth = w_in.shape[0]
    T = B * S
    assert S % TS_MIX == 0 and TS_MIX % CHUNK_L == 0 and S % TM_IN == 0 and T % TM_POST == 0

    w_z, b_z, w_gate, b_gate = _reorder_in_proj(w_in, b_in)
    b_z = b_z[:, None, :]
    pad = GATE_LANES - GATE_ROWS
    w_g = jnp.pad(w_gate, ((0, 0), (0, 0), (0, pad))).astype(BF16)
    b_g = jnp.pad(b_gate, ((0, 0), (0, pad)))[:, None, :]

    pos = jnp.arange(S, dtype=F32)
    inv_freq = ROPE_BASE ** (-jnp.arange(0, R_DK, 2, dtype=F32) / R_DK)
    ang = pos[:, None] * inv_freq[None, :]
    cos = jnp.cos(ang)
    sin = jnp.sin(ang)

    bf = lambda w: w.astype(BF16)
    row = lambda v: v[:, None, :]
    w_bm, w_br, w_out, w_ff1, w_ff2, w_pe_gate, w_pe = map(
        bf, (w_bm, w_br, w_out, w_ff1, w_ff2, w_pe_gate, w_pe))
    norm1_g, conv_b, m_norm_g, r_norm_g, norm2_g, b_ff1, b_ff2, norm3_g = map(
        row, (norm1_g, conv_b, m_norm_g, r_norm_g, norm2_g, b_ff1, b_ff2, norm3_g))
    final_g = final_g[None, :]

    x2d = x.reshape(T, D_MODEL)
    p3d = p.reshape(depth, T, PE_DIM)
    for i in range(depth):
        z, gcol, grow = _inproj(i, S, x2d, norm1_g, w_z, b_z, w_g, b_g,
                                conv_w, conv_b, cos, sin)
        ym, yr = _mixer(i, B, S, z, gcol, grow, m_norm_g, r_norm_g)
        x2d = _merge(i, ym, yr, z, x2d, w_bm, w_br, w_out)
        x2d = _post(i, i == depth - 1, x2d, p3d,
                    norm2_g, w_ff1, b_ff1, w_ff2, b_ff2, norm3_g, w_pe_gate, w_pe, final_g)
    return x2d.reshape(B, S, D_MODEL)
```
